```python
import math
import jax
import jax.numpy as jnp
from jax import lax
import numpy as np

D_MODEL = 1024
BATCH = 8
SEQ = 2048
DEPTH = 4

EXPAND = 2
D_INNER = EXPAND * D_MODEL
N_MIXERS = 4
D_BRANCH = D_INNER // N_MIXERS
CHUNK = 128
CONV_K = 4
NORM_EPS = 1e-6

SSD_HEAD_DIM = 64
SSD_HEADS = D_BRANCH // SSD_HEAD_DIM
SSD_GROUPS = 2
SSD_STATE = 128
SSD_CONV_DIM = D_BRANCH + 2 * SSD_GROUPS * SSD_STATE

ML_HEADS = 4
ML_HEAD_DIM = D_BRANCH // ML_HEADS

S5_GROUP = 16
S5_GROUPS = D_BRANCH // S5_GROUP
S5_STATE = 64

RET_HEADS = 4
RET_QK_DIM = 64
RET_V_DIM = D_BRANCH // RET_HEADS
RET_DECAY_BASE = 5.0
ROPE_BASE = 10000.0

IN_SPLITS = (
    D_BRANCH, SSD_CONV_DIM, SSD_HEADS,
    D_BRANCH, D_BRANCH, D_BRANCH, D_BRANCH, D_BRANCH, ML_HEADS, ML_HEADS,
    D_BRANCH, D_BRANCH,
    D_BRANCH, RET_HEADS * RET_QK_DIM, RET_HEADS * RET_QK_DIM, D_BRANCH,
)
D_IN_PROJ = sum(IN_SPLITS)

F32 = jnp.float32

kernel_name = 'hybrid_ssd_mlstm_s5_retention_trunk'


def rmsnorm(x, w):
    xf = x.astype(F32)
    y = xf * lax.rsqrt(jnp.mean(xf * xf, axis=-1, keepdims=True) + NORM_EPS)
    return y * w.astype(F32)


def head_rmsnorm(y, w):
    y = y * lax.rsqrt(jnp.mean(y * y, axis=-1, keepdims=True) + NORM_EPS)
    return y.reshape(y.shape[0], y.shape[1], -1) * w.astype(F32)


def causal_conv(x, w, b):
    out = lax.conv_general_dilated(
        x, w[:, None, :].astype(x.dtype), window_strides=(1,),
        padding=[(CONV_K - 1, 0)], dimension_numbers=('NWC', 'WIO', 'NWC'),
        feature_group_count=x.shape[-1])
    return out + b.astype(out.dtype)


def rotary(x, positions):
    half = x.shape[-1] // 2
    inv_freq = jnp.exp(-math.log(ROPE_BASE) * jnp.arange(half, dtype=F32) / half)
    ang = positions.astype(F32)[..., None] * inv_freq
    cos = jnp.cos(ang)[:, :, None, :]
    sin = jnp.sin(ang)[:, :, None, :]
    x1, x2 = x[..., :half], x[..., half:]
    return jnp.concatenate([x1 * cos - x2 * sin, x2 * cos + x1 * sin], axis=-1)


def segsum(a):
    t = a.shape[-1]
    cs = jnp.cumsum(a, axis=-1)
    diff = cs[..., :, None] - cs[..., None, :]
    mask = jnp.tril(jnp.ones((t, t), dtype=bool))
    return jnp.where(mask, diff, -jnp.inf)


def ssd_chunked(x, dt, a, bm, cm):
    bsz, s, nh, p = x.shape
    g, n = bm.shape[2], bm.shape[3]
    e = nh // g
    nc = s // CHUNK
    xd = (x * dt[..., None]).reshape(bsz, nc, CHUNK, g, e, p)
    da = (dt * a).reshape(bsz, nc, CHUNK, g, e).transpose(0, 3, 4, 1, 2)
    bc = bm.reshape(bsz, nc, CHUNK, g, n)
    cc = cm.reshape(bsz, nc, CHUNK, g, n)
    a_cum = jnp.cumsum(da, axis=-1)
    decay_in = jnp.exp(segsum(da))
    cb = jnp.einsum('bclgn,bcsgn->bcgls', cc, bc)
    y_diag = jnp.einsum('bcgls,bgecls,bcsgep->bclgep', cb, decay_in, xd)
    decay_to_end = jnp.exp(a_cum[..., -1:] - a_cum)
    states = jnp.einsum('bclgn,bgecl,bclgep->bcgepn', bc, decay_to_end, xd)
    states = jnp.concatenate([jnp.zeros_like(states[:, :1]), states], axis=1)
    chunk_tot = jnp.pad(a_cum[..., -1], ((0, 0), (0, 0), (0, 0), (1, 0)))
    decay_chunk = jnp.exp(segsum(chunk_tot))
    states = jnp.einsum('bgezc,bcgepn->bzgepn', decay_chunk, states)[:, :-1]
    y_off = jnp.einsum('bclgn,bcgepn,bgecl->bclgep', cc, states, jnp.exp(a_cum))
    return (y_diag + y_off).reshape(bsz, s, nh, p)


def ssd_branch(z, xbc, dt_raw, conv_w, conv_b, dt_bias, a_log, d_skip, norm_w):
    bsz, s, _ = xbc.shape
    xbc = jax.nn.silu(causal_conv(xbc, conv_w, conv_b)).astype(F32)
    xs, bm, cm = jnp.split(xbc, [D_BRANCH, D_BRANCH + SSD_GROUPS * SSD_STATE], axis=-1)
    xs = xs.reshape(bsz, s, SSD_HEADS, SSD_HEAD_DIM)
    bm = bm.reshape(bsz, s, SSD_GROUPS, SSD_STATE)
    cm = cm.reshape(bsz, s, SSD_GROUPS, SSD_STATE)
    dt = jax.nn.softplus(dt_raw.astype(F32) + dt_bias.astype(F32))
    a = -jnp.exp(a_log.astype(F32))
    y = ssd_chunked(xs, dt, a, bm, cm) + d_skip.astype(F32)[:, None] * xs
    y = y.reshape(bsz, s, D_BRANCH) * jax.nn.silu(z.astype(F32))
    return rmsnorm(y, norm_w)


def mlstm_chunkwise(q, k, v, i_log, f_log):
    bsz, s, nh, dk = q.shape
    dv = v.shape[-1]
    nc = s // CHUNK
    qc = q.reshape(bsz, nc, CHUNK, nh, dk)
    kc = k.reshape(bsz, nc, CHUNK, nh, dk)
    vc = v.reshape(bsz, nc, CHUNK, nh, dv)
    ic = i_log.reshape(bsz, nc, CHUNK, nh).transpose(0, 3, 1, 2)
    fc = f_log.reshape(bsz, nc, CHUNK, nh).transpose(0, 3, 1, 2)
    bcum = jnp.cumsum(fc, axis=-1)
    b_last = bcum[..., -1]
    a = b_last[..., None] - bcum + ic
    m_loc = jnp.max(a, axis=-1)
    w = jnp.exp(a - m_loc[..., None])
    c_loc = jnp.einsum('bhcl,bclhk,bclhv->bchkv', w, kc, vc)
    n_loc = jnp.einsum('bhcl,bclhk->bchk', w, kc)

    def step(carry, inp):
        c_st, n_st, m_st = carry
        cl, nl, ml, bl = inp
        m_new = jnp.maximum(bl + m_st, ml)
        s_old = jnp.exp(bl + m_st - m_new)
        s_new = jnp.exp(ml - m_new)
        c_new = s_old[..., None, None] * c_st + s_new[..., None, None] * cl
        n_new = s_old[..., None] * n_st + s_new[..., None] * nl
        return (c_new, n_new, m_new), (c_st, n_st, m_st)

    init = (jnp.zeros((bsz, nh, dk, dv), F32), jnp.zeros((bsz, nh, dk), F32),
            jnp.zeros((bsz, nh), F32))
    xs = (jnp.moveaxis(c_loc, 1, 0), jnp.moveaxis(n_loc, 1, 0),
          jnp.moveaxis(m_loc, 2, 0), jnp.moveaxis(b_last, 2, 0))
    _, (c_prev, n_prev, m_prev) = lax.scan(step, init, xs)
    c_prev = jnp.moveaxis(c_prev, 0, 1)
    n_prev = jnp.moveaxis(n_prev, 0, 1)
    m_prev = jnp.moveaxis(m_prev, 0, 2)
    mask = jnp.tril(jnp.ones((CHUNK, CHUNK), dtype=bool))
    dmat = jnp.where(mask, bcum[..., :, None] - bcum[..., None, :] + ic[..., None, :], -jnp.inf)
    g_inter = bcum + m_prev[..., None]
    m_t = jnp.maximum(g_inter, jnp.max(dmat, axis=-1))
    sm = jnp.einsum('bcthk,bcshk->bhcts', qc, kc) * jnp.exp(dmat - m_t[..., None])
    inter = jnp.exp(g_inter - m_t)
    num = (jnp.einsum('bhcts,bcshv->bcthv', sm, vc)
           + jnp.einsum('bcthk,bchkv->bcthv', qc, c_prev) * inter.transpose(0, 2, 3, 1)[..., None])
    den = jnp.sum(sm, axis=-1) + inter * jnp.einsum('bcthk,bchk->bhct', qc, n_prev)
    denom = jnp.maximum(jnp.abs(den), jnp.exp(-m_t)).transpose(0, 2, 3, 1)
    return (num / denom[..., None]).reshape(bsz, s, nh, dv)


def mlstm_branch(z, q_raw, k_raw, v, o_pre, i_pre, f_pre, conv_w, conv_b, i_bias, f_bias, norm_w):
    bsz, s, _ = v.shape
    qk = jax.nn.silu(causal_conv(jnp.concatenate([q_raw, k_raw], axis=-1), conv_w, conv_b)).astype(F32)
    q, k = jnp.split(qk, 2, axis=-1)
    q = q.reshape(bsz, s, ML_HEADS, ML_HEAD_DIM)
    k = k.reshape(bsz, s, ML_HEADS, ML_HEAD_DIM) * ML_HEAD_DIM ** -0.5
    v = v.astype(F32).reshape(bsz, s, ML_HEADS, ML_HEAD_DIM)
    i_log = i_pre.astype(F32) + i_bias.astype(F32)
    f_log = jax.nn.log_sigmoid(f_pre.astype(F32) + f_bias.astype(F32))
    h = mlstm_chunkwise(q, k, v, i_log, f_log)
    h = jax.nn.sigmoid(o_pre.astype(F32)).reshape(bsz, s, ML_HEADS, ML_HEAD_DIM) * h
    return head_rmsnorm(h, norm_w) * jax.nn.silu(z.astype(F32))


def complex_affine_combine(e1, e2):
    a1r, a1i, b1r, b1i = e1
    a2r, a2i, b2r, b2i = e2
    return (a1r * a2r - a1i * a2i, a1r * a2i + a1i * a2r,
            a2r * b1r - a2i * b1i + b2r, a2r * b1i + a2i * b1r + b2i)


def s5_branch(z, u, lam_re, lam_im, b_re, b_im, c_re, c_im, d, log_step, w_glu, b_glu, norm_w):
    bsz, s, _ = u.shape
    u = u.astype(F32)
    step = jnp.exp(log_step.astype(F32))[:, None]
    lr = jnp.minimum(lam_re.astype(F32), -1e-4)
    li = lam_im.astype(F32)
    mag = jnp.exp(lr * step)
    ang = li * step
    ab_re = mag * jnp.cos(ang)
    ab_im = mag * jnp.sin(ang)
    den = lr * lr + li * li
    coef_re = ((ab_re - 1.0) * lr + ab_im * li) / den
    coef_im = (ab_im * lr - (ab_re - 1.0) * li) / den
    br = b_re.astype(F32)
    bi = b_im.astype(F32)
    bb_re = coef_re[..., None] * br - coef_im[..., None] * bi
    bb_im = coef_re[..., None] * bi + coef_im[..., None] * br
    ug = u.reshape(bsz, s, S5_GROUPS, S5_GROUP)
    bu_re = jnp.einsum('gpc,bsgc->bsgp', bb_re, ug)
    bu_im = jnp.einsum('gpc,bsgc->bsgp', bb_im, ug)
    a_re = jnp.broadcast_to(ab_re, bu_re.shape)
    a_im = jnp.broadcast_to(ab_im, bu_im.shape)
    _, _, x_re, x_im = lax.associative_scan(complex_affine_combine, (a_re, a_im, bu_re, bu_im), axis=1)
    y = (jnp.einsum('gcp,bsgp->bsgc', c_re.astype(F32), x_re)
         - jnp.einsum('gcp,bsgp->bsgc', c_im.astype(F32), x_im))
    y = y.reshape(bsz, s, D_BRANCH) + d.astype(F32) * u
    y = jax.nn.gelu(y)
    ga, gb = jnp.split(y @ w_glu.astype(F32) + b_glu.astype(F32), 2, axis=-1)
    y = ga * jax.nn.sigmoid(gb)
    return rmsnorm(y, norm_w) * jax.nn.silu(z.astype(F32))


def retention_chunkwise(q, k, v, log_gamma):
    bsz, s, nh, dk = q.shape
    dv = v.shape[-1]
    nc = s // CHUNK
    qc = q.reshape(bsz, nc, CHUNK, nh, dk)
    kc = k.reshape(bsz, nc, CHUNK, nh, dk)
    vc = v.reshape(bsz, nc, CHUNK, nh, dv)
    pos = jnp.arange(CHUNK, dtype=F32)
    rel = pos[:, None] - pos[None, :]
    decay_in = jnp.where(rel >= 0, jnp.exp(jnp.maximum(rel, 0.0)[None] * log_gamma[:, None, None]), 0.0)
    scores = jnp.einsum('bcthk,bcshk->bchts', qc, kc) * decay_in
    inner = jnp.einsum('bchts,bcshv->bcthv', scores, vc)
    to_end = jnp.exp((CHUNK - 1.0 - pos)[:, None] * log_gamma[None, :])
    s_loc = jnp.einsum('bcshk,sh,bcshv->bchkv', kc, to_end, vc)
    cidx = jnp.arange(nc, dtype=F32)
    cgap = cidx[:, None] - cidx[None, :] - 1.0
    decay_chunk = jnp.where(cgap >= 0, jnp.exp(CHUNK * jnp.maximum(cgap, 0.0)[None] * log_gamma[:, None, None]), 0.0)
    r_start = jnp.einsum('hzc,bchkv->bzhkv', decay_chunk, s_loc)
    from_start = jnp.exp((pos + 1.0)[:, None] * log_gamma[None, :])
    cross = jnp.einsum('bcthk,bchkv->bcthv', qc, r_start) * from_start[:, :, None]
    return (inner + cross).reshape(bsz, s, nh, dv)


def retention_branch(z, q, k, v, positions, norm_w):
    bsz, s, _ = v.shape
    q = rotary(q.astype(F32).reshape(bsz, s, RET_HEADS, RET_QK_DIM), positions)
    k = rotary(k.astype(F32).reshape(bsz, s, RET_HEADS, RET_QK_DIM), positions) * RET_QK_DIM ** -0.5
    v = v.astype(F32).reshape(bsz, s, RET_HEADS, RET_V_DIM)
    log_gamma = jnp.log1p(-jnp.exp2(-(RET_DECAY_BASE + jnp.arange(RET_HEADS, dtype=F32))))
    y = retention_chunkwise(q, k, v, log_gamma)
    return head_rmsnorm(y, norm_w) * jax.nn.silu(z.astype(F32))


def hybrid_layer(x, cond, positions, norm_w, w_ada, b_ada, w_in, w_out,
                 ssd_conv_w, ssd_conv_b, ssd_dt_bias, ssd_a_log, ssd_d, ssd_norm_w,
                 ml_conv_w, ml_conv_b, ml_i_bias, ml_f_bias, ml_norm_w,
                 s5_lambda_re, s5_lambda_im, s5_b_re, s5_b_im, s5_c_re, s5_c_im,
                 s5_d, s5_log_step, s5_w_glu, s5_b_glu, s5_norm_w, ret_norm_w):
    mod = cond @ w_ada.astype(F32) + b_ada.astype(F32)
    shift, scale, gate = jnp.split(mod, 3, axis=-1)
    hn = (rmsnorm(x, norm_w) * (1.0 + scale[:, None, :]) + shift[:, None, :]).astype(x.dtype)
    proj = hn @ w_in
    split_points = np.cumsum(IN_SPLITS)[:-1].tolist()
    (s_z, s_xbc, s_dt, m_z, m_q, m_k, m_v, m_o, m_i, m_f,
     c_z, c_u, r_z, r_q, r_k, r_v) = jnp.split(proj, split_points, axis=-1)
    y_ssd = ssd_branch(s_z, s_xbc, s_dt, ssd_conv_w, ssd_conv_b, ssd_dt_bias, ssd_a_log, ssd_d, ssd_norm_w)
    y_ml = mlstm_branch(m_z, m_q, m_k, m_v, m_o, m_i, m_f, ml_conv_w, ml_conv_b, ml_i_bias, ml_f_bias, ml_norm_w)
    y_s5 = s5_branch(c_z, c_u, s5_lambda_re, s5_lambda_im, s5_b_re, s5_b_im, s5_c_re, s5_c_im,
                     s5_d, s5_log_step, s5_w_glu, s5_b_glu, s5_norm_w)
    y_ret = retention_branch(r_z, r_q, r_k, r_v, positions, ret_norm_w)
    y = jnp.concatenate([y_ssd, y_ml, y_s5, y_ret], axis=-1).astype(x.dtype)
    out = y @ w_out
    return (x + gate[:, None, :] * out).astype(x.dtype)


def setup_inputs(seed: int = 0) -> dict:
    key = jax.random.key(seed)
    ks = iter(jax.random.split(key, 48))
    L = DEPTH

    def nrm(shape, scale):
        return scale * jax.random.normal(next(ks), shape, F32)

    def gain(shape):
        return 1.0 + nrm(shape, 0.02)

    def log_uniform(shape, lo, hi):
        return jax.random.uniform(next(ks), shape, F32, minval=math.log(lo), maxval=math.log(hi))

    x = nrm((BATCH, SEQ, D_MODEL), 1.0)
    c = nrm((BATCH, D_MODEL), 1.0)
    offset = jax.random.randint(next(ks), (BATCH, 1), 0, 4096, dtype=jnp.int32)
    positions = offset + jnp.arange(SEQ, dtype=jnp.int32)[None, :]

    dt0 = jnp.exp(log_uniform((L, SSD_HEADS), 1e-3, 1e-1))
    ssd_dt_bias = dt0 + jnp.log(-jnp.expm1(-dt0))
    ssd_a_log = jnp.log(jax.random.uniform(next(ks), (L, SSD_HEADS), F32, minval=1.0, maxval=16.0))

    s5_lambda_re = -0.5 + nrm((L, S5_GROUPS, S5_STATE), 0.01)
    s5_lambda_im = (jnp.pi * jnp.arange(S5_STATE, dtype=F32))[None, None, :] + nrm((L, S5_GROUPS, S5_STATE), 0.01)

    return {
        'x': x,
        'c': c,
        'positions': positions,
        'norm_w': gain((L, D_MODEL)),
        'w_ada': nrm((L, D_MODEL, 3 * D_MODEL), 0.3 * D_MODEL ** -0.5),
        'b_ada': nrm((L, 3 * D_MODEL), 0.02),
        'w_in': nrm((L, D_MODEL, D_IN_PROJ), D_MODEL ** -0.5),
        'w_out': nrm((L, D_INNER, D_MODEL), D_INNER ** -0.5),
        'ssd_conv_w': nrm((L, CONV_K, SSD_CONV_DIM), CONV_K ** -0.5),
        'ssd_conv_b': nrm((L, SSD_CONV_DIM), 0.02),
        'ssd_dt_bias': ssd_dt_bias,
        'ssd_a_log': ssd_a_log,
        'ssd_d': gain((L, SSD_HEADS)),
        'ssd_norm_w': gain((L, D_BRANCH)),
        'ml_conv_w': nrm((L, CONV_K, 2 * D_BRANCH), CONV_K ** -0.5),
        'ml_conv_b': nrm((L, 2 * D_BRANCH), 0.02),
        'ml_i_bias': nrm((L, ML_HEADS), 0.1),
        'ml_f_bias': jnp.linspace(3.0, 6.0, ML_HEADS, dtype=F32)[None, :] + nrm((L, ML_HEADS), 0.1),
        'ml_norm_w': gain((L, D_BRANCH)),
        's5_lambda_re': s5_lambda_re,
        's5_lambda_im': s5_lambda_im,
        's5_b_re': nrm((L, S5_GROUPS, S5_STATE, S5_GROUP), S5_GROUP ** -0.5),
        's5_b_im': nrm((L, S5_GROUPS, S5_STATE, S5_GROUP), S5_GROUP ** -0.5),
        's5_c_re': nrm((L, S5_GROUPS, S5_GROUP, S5_STATE), S5_STATE ** -0.5),
        's5_c_im': nrm((L, S5_GROUPS, S5_GROUP, S5_STATE), S5_STATE ** -0.5),
        's5_d': nrm((L, D_BRANCH), 0.5),
        's5_log_step': log_uniform((L, S5_GROUPS), 1e-3, 1e-1),
        's5_w_glu': nrm((L, D_BRANCH, 2 * D_BRANCH), D_BRANCH ** -0.5),
        's5_b_glu': nrm((L, 2 * D_BRANCH), 0.02),
        's5_norm_w': gain((L, D_BRANCH)),
        'ret_norm_w': gain((L, D_BRANCH)),
        'final_norm_w': gain((D_MODEL,)),
    }


def reference(x, c, positions, norm_w, w_ada, b_ada, w_in, w_out,
              ssd_conv_w, ssd_conv_b, ssd_dt_bias, ssd_a_log, ssd_d, ssd_norm_w,
              ml_conv_w, ml_conv_b, ml_i_bias, ml_f_bias, ml_norm_w,
              s5_lambda_re, s5_lambda_im, s5_b_re, s5_b_im, s5_c_re, s5_c_im,
              s5_d, s5_log_step, s5_w_glu, s5_b_glu, s5_norm_w, ret_norm_w, final_norm_w):
    cond = jax.nn.silu(c.astype(F32))
    h = x
    for l in range(DEPTH):
        h = hybrid_layer(h, cond, positions, norm_w[l], w_ada[l], b_ada[l], w_in[l], w_out[l],
                         ssd_conv_w[l], ssd_conv_b[l], ssd_dt_bias[l], ssd_a_log[l], ssd_d[l], ssd_norm_w[l],
                         ml_conv_w[l], ml_conv_b[l], ml_i_bias[l], ml_f_bias[l], ml_norm_w[l],
                         s5_lambda_re[l], s5_lambda_im[l], s5_b_re[l], s5_b_im[l], s5_c_re[l], s5_c_im[l],
                         s5_d[l], s5_log_step[l], s5_w_glu[l], s5_b_glu[l], s5_norm_w[l], ret_norm_w[l])
    return rmsnorm(h, final_norm_w).astype(x.dtype)
```

```python
import functools
import math

import numpy as np
import jax
import jax.numpy as jnp
from jax import lax
from jax.experimental import pallas as pl
from jax.experimental.pallas import tpu as pltpu

F32 = jnp.float32
BF16 = jnp.bfloat16

D_MODEL = 1024
D_BRANCH = 512
CHUNK = 128
CONV_K = 4
CONV_PAD = 8
NORM_EPS = 1e-6
NEG_BIG = -1e30

SSD_HEADS = 8
SSD_HEAD_DIM = 64
SSD_GROUPS = 2
SSD_STATE = 128
ML_HEADS = 4
ML_HEAD_DIM = 128
S5_GROUPS = 32
S5_GROUP = 16
S5_STATE = 64
S5_SUB = 8
S5_LANE_BLOCKS = 4
RET_HEADS = 4
RET_QK = 64
RET_V = 128
RET_DECAY_BASE = 5.0
ROPE_BASE = 10000.0
LANES = 128

_O_SSD_Z, _O_SSD_XBC, _O_SSD_DT = 0, 512, 1536
_O_ML_Z, _O_ML_I, _O_ML_F = 1544, 4104, 4108
_O_S5_Z = 4112
_O_RET_Z = 5136
_D_IN = 6672
_W_SSD, _W_ML, _W_S5, _W_RET, _W_SM = 1536, 2560, 1024, 1536, 384

VMEM_LIMIT = 56 * 1024 * 1024


def _cp(sem):
    return pltpu.CompilerParams(dimension_semantics=sem, vmem_limit_bytes=VMEM_LIMIT)


def _dot(a, b):
    return jnp.dot(a, b, preferred_element_type=F32)


def _dot_nt(a, b):
    return lax.dot_general(a, b, (((1,), (1,)), ((), ())), preferred_element_type=F32)


def _dot_tn(a, b):
    return lax.dot_general(a, b, (((0,), (0,)), ((), ())), preferred_element_type=F32)


def _sigmoid(x):
    return 1.0 / (1.0 + jnp.exp(-x))


def _silu(x):
    return x * _sigmoid(x)


def _softplus(x):
    return jnp.maximum(x, 0.0) + jnp.log1p(jnp.exp(-jnp.abs(x)))


def _split3(v):
    hi = v.astype(BF16)
    r1 = v - hi.astype(F32)
    mid = r1.astype(BF16)
    lo = (r1 - mid.astype(F32)).astype(BF16)
    return hi, mid, lo


def _dot01_l(m01, v):
    hi, mid, lo = _split3(v)
    return _dot(m01, hi) + _dot(m01, mid) + _dot(m01, lo)


def _dot01_r(v, m01):
    hi, mid, lo = _split3(v)
    return _dot(hi, m01) + _dot(mid, m01) + _dot(lo, m01)


def _tri_mask():
    row = lax.broadcasted_iota(jnp.int32, (CHUNK, CHUNK), 0)
    col = lax.broadcasted_iota(jnp.int32, (CHUNK, CHUNK), 1)
    return row >= col


def _causal_conv(x_ref_rows, buf_ref, cw_ref, cb_ref):
    n = x_ref_rows.shape[0]
    buf_ref[CONV_PAD:CONV_PAD + n, :] = x_ref_rows
    acc = jnp.broadcast_to(cb_ref[...], x_ref_rows.shape)
    for k in range(CONV_K):
        off = CONV_PAD - (CONV_K - 1) + k
        acc = acc + cw_ref[k:k + 1, :] * buf_ref[off:off + n, :]
    buf_ref[0:CONV_PAD, :] = buf_ref[n:n + CONV_PAD, :]
    return acc


def _mod_kernel(c_ref, w_ref, b_ref, o_ref):
    cond = _silu(c_ref[...])
    o_ref[0] = _dot(cond.astype(BF16), w_ref[0].astype(BF16)) + b_ref[0]


def _mod_call(c, w_ada, b_ada):
    depth, d, n = w_ada.shape
    bsz = c.shape[0]
    tn = 1024
    return pl.pallas_call(
        _mod_kernel,
        grid=(depth, n // tn),
        in_specs=[
            pl.BlockSpec((bsz, d), lambda l, j: (0, 0)),
            pl.BlockSpec((1, d, tn), lambda l, j: (l, 0, j)),
            pl.BlockSpec((1, 1, tn), lambda l, j: (l, 0, j)),
        ],
        out_specs=pl.BlockSpec((1, bsz, tn), lambda l, j: (l, 0, j)),
        out_shape=jax.ShapeDtypeStruct((depth, bsz, n), F32),
        compiler_params=_cp(("parallel", "parallel")),
        name="adaln_mod",
    )(c, w_ada, b_ada.reshape(depth, 1, n))


def _rope_kernel(pos_ref, f_ref, sg_ref, cos_ref, sin_ref):
    ang = pos_ref[...] * f_ref[...]
    cos_ref[...] = jnp.cos(ang)
    sin_ref[...] = jnp.sin(ang) * sg_ref[...]


def _rope_call(positions):
    bsz, s = positions.shape
    m = bsz * s
    half = RET_QK // 2
    inv_freq = jnp.exp(-math.log(ROPE_BASE) * jnp.arange(half, dtype=F32) / half)
    f128 = jnp.tile(inv_freq, LANES // half).reshape(1, LANES)
    sign = np.tile(np.concatenate([-np.ones(half), np.ones(half)]), LANES // RET_QK)
    sign = jnp.asarray(sign.reshape(1, LANES), F32)
    pos = jnp.broadcast_to(positions.astype(F32).reshape(m, 1), (m, LANES))
    tm = min(1024, m)
    return pl.pallas_call(
        _rope_kernel,
        grid=(m // tm,),
        in_specs=[
            pl.BlockSpec((tm, LANES), lambda i: (i, 0)),
            pl.BlockSpec((1, LANES), lambda i: (0, 0)),
            pl.BlockSpec((1, LANES), lambda i: (0, 0)),
        ],
        out_specs=[pl.BlockSpec((tm, LANES), lambda i: (i, 0))] * 2,
        out_shape=[jax.ShapeDtypeStruct((m, LANES), F32)] * 2,
        compiler_params=_cp(("parallel",)),
        name="rope_tables",
    )(pos, f128, sign)


def _inproj_kernel(h_ref, nw_ref, sc_ref, sh_ref, w1, w2, w3, w4, w5, o1, o2, o3, o4, o5):
    x = h_ref[...]
    ms = jnp.mean(x * x, axis=-1, keepdims=True)
    hn = x * lax.rsqrt(ms + NORM_EPS) * nw_ref[...]
    hn = hn * (1.0 + sc_ref[0]) + sh_ref[0]
    hb = hn.astype(BF16)
    for w, o in ((w1, o1), (w2, o2), (w3, o3), (w4, o4), (w5, o5)):
        o[...] = _dot(hb, w[...]).astype(o.dtype)


def _inproj_call(h, norm_w, scale, shift, weights, s):
    m = h.shape[0]
    tm = min(512, s)
    per_batch = s // tm
    widths = (_W_SSD, _W_ML, _W_S5, _W_RET, _W_SM)
    dtypes = (BF16, BF16, BF16, BF16, F32)
    w_specs = [pl.BlockSpec((D_MODEL, n), lambda i: (0, 0), pipeline_mode=pl.Buffered(1))
               for n in widths]
    return pl.pallas_call(
        _inproj_kernel,
        grid=(m // tm,),
        in_specs=[
            pl.BlockSpec((tm, D_MODEL), lambda i: (i, 0)),
            pl.BlockSpec((1, D_MODEL), lambda i: (0, 0)),
            pl.BlockSpec((1, 1, D_MODEL), lambda i: (i // per_batch, 0, 0)),
            pl.BlockSpec((1, 1, D_MODEL), lambda i: (i // per_batch, 0, 0)),
        ] + w_specs,
        out_specs=[pl.BlockSpec((tm, n), lambda i: (i, 0)) for n in widths],
        out_shape=[jax.ShapeDtypeStruct((m, n), dt) for n, dt in zip(widths, dtypes)],
        compiler_params=_cp(("parallel",)),
        name="in_proj",
    )(h, norm_w, scale, shift, *weights)


def _ssd_kernel(p_ref, sm_ref, cw_ref, cb_ref, dtb_ref, alog_ref, dsk_ref, nw_ref, e8_ref,
                o_ref, buf_ref, st_ref):
    @pl.when(pl.program_id(1) == 0)
    def _():
        buf_ref[0:CONV_PAD, :] = jnp.zeros((CONV_PAD, buf_ref.shape[1]), F32)
        st_ref[...] = jnp.zeros(st_ref.shape, F32)

    tri = _tri_mask()
    tri_bf = jnp.where(tri, 1.0, 0.0).astype(BF16)
    e8 = e8_ref[...]

    z = p_ref[:, 0:D_BRANCH].astype(F32)
    xbc = p_ref[:, D_BRANCH:].astype(F32)
    xa = _silu(_causal_conv(xbc, buf_ref, cw_ref, cb_ref))
    xs = xa[:, 0:D_BRANCH]
    gw = SSD_GROUPS * SSD_STATE
    bm = xa[:, D_BRANCH:D_BRANCH + gw].astype(BF16)
    cm = xa[:, D_BRANCH + gw:D_BRANCH + 2 * gw].astype(BF16)

    dt = _softplus(sm_ref[:, 0:LANES] + dtb_ref[...])
    da = dt * (-jnp.exp(alog_ref[...]))
    a_cum = _dot01_l(tri_bf, da)
    a_cum_t = a_cum.T
    ac_e = _dot01_r(a_cum, e8)
    dt_e = _dot01_r(dt, e8)
    a_last_e = ac_e[CHUNK - 1:CHUNK, :]
    xd = xs * dt_e
    xd_bf = xd.astype(BF16)
    xw_bf = (xd * jnp.exp(a_last_e - ac_e)).astype(BF16)
    ea_e = jnp.exp(ac_e)

    hw = (SSD_HEADS // SSD_GROUPS) * SSD_HEAD_DIM
    lane = lax.broadcasted_iota(jnp.int32, (CHUNK, hw), 1)
    ys = []
    for g in range(SSD_GROUPS):
        bg = bm[:, g * SSD_STATE:(g + 1) * SSD_STATE]
        cg = cm[:, g * SSD_STATE:(g + 1) * SSD_STATE]
        cb = _dot_nt(cg, bg)
        s_prev = st_ref[g]
        y = _dot(cg, s_prev.astype(BF16)) * ea_e[:, g * hw:(g + 1) * hw]
        xd_g = xd_bf[:, g * hw:(g + 1) * hw]
        for hh in range(SSD_HEADS // SSD_GROUPS):
            h = g * (SSD_HEADS // SSD_GROUPS) + hh
            seg = a_cum[:, h:h + 1] - a_cum_t[h:h + 1, :]
            dec = jnp.exp(jnp.where(tri, seg, NEG_BIG))
            m_h = (cb * dec).astype(BF16)
            in_head = (lane >= hh * SSD_HEAD_DIM) & (lane < (hh + 1) * SSD_HEAD_DIM)
            y = y + _dot(m_h, jnp.where(in_head, xd_g, jnp.zeros_like(xd_g)))
        ys.append(y)
        s_loc = _dot_tn(bg, xw_bf[:, g * hw:(g + 1) * hw])
        st_ref[g] = s_prev * jnp.exp(a_last_e[:, g * hw:(g + 1) * hw]) + s_loc

    y = jnp.concatenate(ys, axis=1) + dsk_ref[...] * xs
    y = y * _silu(z)
    ms = jnp.mean(y * y, axis=-1, keepdims=True)
    o_ref[...] = (y * lax.rsqrt(ms + NORM_EPS) * nw_ref[...]).astype(o_ref.dtype)


def _ssd_call(p_ssd, p_sm, conv_w, conv_b, dt_bias, a_log, d_skip, norm_w, bsz, s):
    nc = s // CHUNK
    m = bsz * s
    cdim = _W_SSD - D_BRANCH
    pad = lambda v: jnp.zeros((1, LANES), F32).at[0, :v.shape[0]].set(v)
    e8 = np.zeros((LANES, D_BRANCH), np.float32)
    for h in range(SSD_HEADS):
        e8[h, h * SSD_HEAD_DIM:(h + 1) * SSD_HEAD_DIM] = 1.0
    const = lambda shape: pl.BlockSpec(shape, lambda b, c: (0,) * len(shape))
    row = lambda n: pl.BlockSpec((CHUNK, n), lambda b, c: (b * nc + c, 0))
    return pl.pallas_call(
        _ssd_kernel,
        grid=(bsz, nc),
        in_specs=[row(_W_SSD), row(LANES),
                  const((CONV_K, cdim)), const((1, cdim)), const((1, LANES)), const((1, LANES)),
                  const((1, D_BRANCH)), const((1, D_BRANCH)), const((LANES, D_BRANCH))],
        out_specs=row(D_BRANCH),
        out_shape=jax.ShapeDtypeStruct((m, D_BRANCH), BF16),
        scratch_shapes=[pltpu.VMEM((CHUNK + CONV_PAD, cdim), F32),
                        pltpu.VMEM((SSD_GROUPS, SSD_STATE, 256), F32)],
        compiler_params=_cp(("parallel", "arbitrary")),
        name="ssd_mixer",
    )(p_ssd, p_sm, conv_w, conv_b.reshape(1, cdim), pad(dt_bias), pad(a_log),
      jnp.repeat(d_skip, SSD_HEAD_DIM).reshape(1, D_BRANCH), norm_w.reshape(1, D_BRANCH),
      jnp.asarray(e8, BF16))


def _mlstm_kernel(p_ref, sm_ref, cw_ref, cb_ref, ib_ref, fb_ref, nw_ref,
                  o_ref, buf_ref, c_ref, m_ref):
    @pl.when(pl.program_id(1) == 0)
    def _():
        buf_ref[0:CONV_PAD, :] = jnp.zeros((CONV_PAD, buf_ref.shape[1]), F32)
        c_ref[...] = jnp.zeros(c_ref.shape, F32)
        m_ref[...] = jnp.zeros(m_ref.shape, F32)

    tri = _tri_mask()
    tri_bf = jnp.where(tri, 1.0, 0.0).astype(BF16)
    d = D_BRANCH

    z = p_ref[:, 0:d].astype(F32)
    qk = _silu(_causal_conv(p_ref[:, d:3 * d].astype(F32), buf_ref, cw_ref, cb_ref))
    q = qk[:, 0:d].astype(BF16)
    k = qk[:, d:2 * d] * (ML_HEAD_DIM ** -0.5)
    v = p_ref[:, 3 * d:4 * d]
    o_gate = _sigmoid(p_ref[:, 4 * d:5 * d].astype(F32))

    i_log = sm_ref[:, LANES:2 * LANES] + ib_ref[...]
    f_log = -_softplus(-(sm_ref[:, 2 * LANES:3 * LANES] + fb_ref[...]))
    f_cum = _dot01_l(tri_bf, f_log)
    r = i_log - f_cum
    r_t = r.T
    f_last = f_cum[CHUNK - 1:CHUNK, :]
    m_prev = m_ref[...]
    a_loc = f_last + r
    m_new = jnp.maximum(f_last + m_prev, jnp.max(a_loc, axis=0, keepdims=True))
    w_loc = jnp.exp(a_loc - m_new)
    s_old = jnp.exp(f_last + m_prev - m_new)
    m_ref[...] = m_new

    ones = jnp.ones((CHUNK, ML_HEAD_DIM), BF16)
    hs = []
    for h in range(ML_HEADS):
        sl = slice(h * ML_HEAD_DIM, (h + 1) * ML_HEAD_DIM)
        qh = q[:, sl]
        kh = k[:, sl]
        v_ext = jnp.concatenate([v[:, sl], ones], axis=1)
        g_col = f_cum[:, h:h + 1]
        dmat = jnp.where(tri, g_col + r_t[h:h + 1, :], NEG_BIG)
        g_inter = g_col + m_prev[:, h:h + 1]
        m_t = jnp.maximum(g_inter, jnp.max(dmat, axis=-1, keepdims=True))
        sm = (_dot_nt(qh, kh.astype(BF16)) * jnp.exp(dmat - m_t)).astype(BF16)
        c_prev = c_ref[h]
        numden = _dot(sm, v_ext) + jnp.exp(g_inter - m_t) * _dot(qh, c_prev.astype(BF16))
        num = numden[:, 0:ML_HEAD_DIM]
        den = numden[:, ML_HEAD_DIM:]
        hs.append(num / jnp.maximum(jnp.abs(den), jnp.exp(-m_t)))
        kw = (kh * w_loc[:, h:h + 1]).astype(BF16)
        c_ref[h] = s_old[:, h:h + 1] * c_prev + _dot_tn(kw, v_ext)

    outs = []
    for h in range(ML_HEADS):
        sl = slice(h * ML_HEAD_DIM, (h + 1) * ML_HEAD_DIM)
        y = o_gate[:, sl] * hs[h]
        ms = jnp.mean(y * y, axis=-1, keepdims=True)
        outs.append(y * lax.rsqrt(ms + NORM_EPS))
    y = jnp.concatenate(outs, axis=1) * nw_ref[...] * _silu(z)
    o_ref[...] = y.astype(o_ref.dtype)


def _mlstm_call(p_ml, p_sm, conv_w, conv_b, i_bias, f_bias, norm_w, bsz, s):
    nc = s // CHUNK
    m = bsz * s
    cdim = 2 * D_BRANCH
    pad = lambda v: jnp.zeros((1, LANES), F32).at[0, :v.shape[0]].set(v)
    const = lambda shape: pl.BlockSpec(shape, lambda b, c: (0,) * len(shape))
    row = lambda n: pl.BlockSpec((CHUNK, n), lambda b, c: (b * nc + c, 0))
    return pl.pallas_call(
        _mlstm_kernel,
        grid=(bsz, nc),
        in_specs=[row(_W_ML), row(_W_SM),
                  const((CONV_K, cdim)), const((1, cdim)), const((1, LANES)), const((1, LANES)),
                  const((1, D_BRANCH))],
        out_specs=row(D_BRANCH),
        out_shape=jax.ShapeDtypeStruct((m, D_BRANCH), BF16),
        scratch_shapes=[pltpu.VMEM((CHUNK + CONV_PAD, cdim), F32),
                        pltpu.VMEM((ML_HEADS, ML_HEAD_DIM, 2 * ML_HEAD_DIM), F32),
                        pltpu.VMEM((1, LANES), F32)],
        compiler_params=_cp(("parallel", "arbitrary")),
        name="mlstm_mixer",
    )(p_ml, p_sm, conv_w, conv_b.reshape(1, cdim), pad(i_bias), pad(f_bias),
      norm_w.reshape(1, D_BRANCH))


def _ret_log_gamma():
    return [math.log1p(-2.0 ** (-(RET_DECAY_BASE + h))) for h in range(RET_HEADS)]


def _ret_consts():
    lg = _ret_log_gamma()
    pos = np.arange(CHUNK, dtype=np.float64)
    rel = pos[:, None] - pos[None, :]
    dm = np.stack([np.where(rel >= 0, np.exp(np.maximum(rel, 0.0) * g), 0.0) for g in lg])
    from_start = np.concatenate(
        [np.repeat(np.exp((pos + 1.0) * g)[:, None], RET_V, axis=1) for g in lg], axis=1)
    to_end = np.concatenate(
        [np.repeat(np.exp((CHUNK - 1.0 - pos) * g)[:, None], RET_QK, axis=1) for g in lg], axis=1)
    qw, vw = RET_HEADS * RET_QK, RET_HEADS * RET_V
    chunk_decay = np.zeros((qw, vw))
    for h, g in enumerate(lg):
        chunk_decay[h * RET_QK:(h + 1) * RET_QK, h * RET_V:(h + 1) * RET_V] = math.exp(CHUNK * g)
    block_diag = (chunk_decay > 0).astype(np.float64)
    swap = np.zeros((qw, qw))
    for j in range(qw):
        base, off = (j // RET_QK) * RET_QK, j % RET_QK
        swap[base + (off + RET_QK // 2) % RET_QK, j] = 1.0
    f = lambda a: jnp.asarray(a, F32)
    return f(dm), f(from_start), f(to_end), f(chunk_decay), f(block_diag), jnp.asarray(swap, BF16)


def _ret_kernel(p_ref, cos_ref, sin_ref, dm_ref, fs_ref, te_ref, cd_ref, bd_ref, sw_ref, nw_ref,
                o_ref, r_ref):
    @pl.when(pl.program_id(1) == 0)
    def _():
        r_ref[...] = jnp.zeros(r_ref.shape, F32)

    d = D_BRANCH
    qw = RET_HEADS * RET_QK
    z = p_ref[:, 0:d].astype(F32)
    q_raw = p_ref[:, d:d + qw]
    k_raw = p_ref[:, d + qw:d + 2 * qw]
    v = p_ref[:, d + 2 * qw:2 * d + 2 * qw]
    cos_e = jnp.concatenate([cos_ref[...]] * (qw // LANES), axis=1)
    sin_e = jnp.concatenate([sin_ref[...]] * (qw // LANES), axis=1)
    sw = sw_ref[...]
    q = q_raw.astype(F32) * cos_e + _dot(q_raw, sw) * sin_e
    k = (k_raw.astype(F32) * cos_e + _dot(k_raw, sw) * sin_e) * (RET_QK ** -0.5)
    k_bf = k.astype(BF16)

    r_prev = r_ref[...]
    y = _dot(q.astype(BF16), r_prev.astype(BF16)) * fs_ref[...]
    lane = lax.broadcasted_iota(jnp.int32, (CHUNK, qw), 1)
    inner = []
    for h in range(RET_HEADS):
        in_head = (lane >= h * RET_QK) & (lane < (h + 1) * RET_QK)
        qh = jnp.where(in_head, q, jnp.zeros_like(q)).astype(BF16)
        scores = (_dot_nt(qh, k_bf) * dm_ref[h]).astype(BF16)
        inner.append(_dot(scores, v[:, h * RET_V:(h + 1) * RET_V]))
    y = y + jnp.concatenate(inner, axis=1)
    r_ref[...] = r_prev * cd_ref[...] + _dot_tn((k * te_ref[...]).astype(BF16), v) * bd_ref[...]

    outs = []
    for h in range(RET_HEADS):
        yh = y[:, h * RET_V:(h + 1) * RET_V]
        ms = jnp.mean(yh * yh, axis=-1, keepdims=True)
        outs.append(yh * lax.rsqrt(ms + NORM_EPS))
    o_ref[...] = (jnp.concatenate(outs, axis=1) * nw_ref[...] * _silu(z)).astype(o_ref.dtype)


def _ret_call(p_ret, cos_t, sin_t, norm_w, bsz, s):
    nc = s // CHUNK
    m = bsz * s
    qw, vw = RET_HEADS * RET_QK, RET_HEADS * RET_V
    dm, fs, te, cd, bd, sw = _ret_consts()
    const = lambda shape: pl.BlockSpec(shape, lambda b, c: (0,) * len(shape))
    row = lambda n: pl.BlockSpec((CHUNK, n), lambda b, c: (b * nc + c, 0))
    return pl.pallas_call(
        _ret_kernel,
        grid=(bsz, nc),
        in_specs=[row(_W_RET), row(LANES), row(LANES),
                  const((RET_HEADS, CHUNK, CHUNK)), const((CHUNK, vw)), const((CHUNK, qw)),
                  const((qw, vw)), const((qw, vw)), const((qw, qw)), const((1, D_BRANCH))],
        out_specs=row(D_BRANCH),
        out_shape=jax.ShapeDtypeStruct((m, D_BRANCH), BF16),
        scratch_shapes=[pltpu.VMEM((qw, vw), F32)],
        compiler_params=_cp(("parallel", "arbitrary")),
        name="retention_mixer",
    )(p_ret, cos_t, sin_t, dm, fs, te, cd, bd, sw, norm_w.reshape(1, D_BRANCH))


def _s5_matrices(lam_re, lam_im, b_re, b_im, c_re, c_im, log_step):
    step = jnp.exp(log_step.astype(F32))[:, None]
    lr = jnp.minimum(lam_re.astype(F32), -1e-4)
    li = lam_im.astype(F32)
    mag = jnp.exp(lr * step)
    ang = li * step
    ab_re = mag * jnp.cos(ang)
    ab_im = mag * jnp.sin(ang)
    den = lr * lr + li * li
    coef_re = ((ab_re - 1.0) * lr + ab_im * li) / den
    coef_im = (ab_im * lr - (ab_re - 1.0) * li) / den
    br, bi = b_re.astype(F32), b_im.astype(F32)
    bb_re = coef_re[..., None] * br - coef_im[..., None] * bi
    bb_im = coef_re[..., None] * bi + coef_im[..., None] * br
    pw_re, pw_im = [jnp.ones_like(ab_re)], [jnp.zeros_like(ab_im)]
    for _ in range(S5_SUB):
        pr, pi = pw_re[-1], pw_im[-1]
        pw_re.append(pr * ab_re - pi * ab_im)
        pw_im.append(pr * ab_im + pi * ab_re)
    pw_re, pw_im = jnp.stack(pw_re), jnp.stack(pw_im)
    cr, ci = c_re.astype(F32), c_im.astype(F32)
    hp = lax.Precision.HIGHEST
    ab_b_re = pw_re[..., None] * bb_re - pw_im[..., None] * bb_im
    ab_b_im = pw_re[..., None] * bb_im + pw_im[..., None] * bb_re
    ca_re = cr[None] * pw_re[:, :, None, :] - ci[None] * pw_im[:, :, None, :]
    ca_im = cr[None] * pw_im[:, :, None, :] + ci[None] * pw_re[:, :, None, :]
    taps = (jnp.einsum('gop,kgpi->kgoi', cr, ab_b_re[:S5_SUB], precision=hp)
            - jnp.einsum('gop,kgpi->kgoi', ci, ab_b_im[:S5_SUB], precision=hp))
    ngl = S5_GROUPS // S5_LANE_BLOCKS
    nb = S5_LANE_BLOCKS
    eye = jnp.eye(ngl, dtype=F32)
    min_re = jnp.stack([ab_b_re[S5_SUB - 1 - t] for t in range(S5_SUB)])
    min_im = jnp.stack([ab_b_im[S5_SUB - 1 - t] for t in range(S5_SUB)])

    def blk_in(a):
        a = a.reshape(S5_SUB, nb, ngl, S5_STATE, S5_GROUP)
        out = jnp.einsum('tbgpi,gh->btgihp', a, eye)
        return out.reshape(nb, S5_SUB * ngl * S5_GROUP, ngl * S5_STATE)

    m_in = jnp.concatenate([blk_in(min_re), blk_in(min_im)], axis=-1)
    toe = jnp.stack([jnp.stack([taps[t2 - t1] if t2 >= t1 else jnp.zeros_like(taps[0])
                                for t2 in range(S5_SUB)]) for t1 in range(S5_SUB)])
    toe = toe.reshape(S5_SUB, S5_SUB, nb, ngl, S5_GROUP, S5_GROUP)
    toe = jnp.einsum('stbgoi,gh->bsgitho', toe, eye)
    toe = toe.reshape(nb, S5_SUB * ngl * S5_GROUP, S5_SUB * ngl * S5_GROUP)

    def blk_out(a):
        a = a.reshape(S5_SUB, nb, ngl, S5_GROUP, S5_STATE)
        out = jnp.einsum('tbgop,gh->bgptho', a, eye)
        return out.reshape(nb, ngl * S5_STATE, S5_SUB * ngl * S5_GROUP)

    m_out = jnp.concatenate([toe, blk_out(ca_re[1:]), -blk_out(ca_im[1:])], axis=1)
    a8 = jnp.concatenate([pw_re[S5_SUB].reshape(nb, 1, ngl * S5_STATE),
                          pw_im[S5_SUB].reshape(nb, 1, ngl * S5_STATE)], axis=-1)
    return m_in.astype(BF16), m_out.astype(BF16), a8


def _s5_scan_kernel(u_ref, min_ref, mout_ref, a8_ref, o_ref, uf_ref, xl_ref, xin_ref, ys_ref):
    s = u_ref.shape[0]
    j = s // S5_SUB
    half = xl_ref.shape[1] // 2
    uf_ref[...] = u_ref[...].astype(F32)
    ub = jnp.concatenate([uf_ref[pl.ds(t, j, stride=S5_SUB), :] for t in range(S5_SUB)],
                         axis=1).astype(BF16)
    xl_ref[...] = _dot(ub, min_ref[0])
    a_re = a8_ref[0, :, 0:half]
    a_im = a8_ref[0, :, half:]

    def body(g, carry):
        x_re, x_im = carry
        r0 = pl.multiple_of(g * 8, 8)
        loc = xl_ref[pl.ds(r0, 8), :]
        rows = []
        for r in range(8):
            rows.append(jnp.concatenate([x_re, x_im], axis=1))
            n_re = a_re * x_re - a_im * x_im + loc[r:r + 1, 0:half]
            n_im = a_re * x_im + a_im * x_re + loc[r:r + 1, half:]
            x_re, x_im = n_re, n_im
        xin_ref[pl.ds(r0, 8), :] = jnp.concatenate(rows, axis=0)
        return x_re, x_im

    zero = jnp.zeros((1, half), F32)
    lax.fori_loop(0, j // 8, body, (zero, zero))
    lhs = jnp.concatenate([ub, xin_ref[...].astype(BF16)], axis=1)
    y = _dot(lhs, mout_ref[0])
    for t in range(S5_SUB):
        ys_ref[pl.ds(t, j, stride=S5_SUB), :] = y[:, t * LANES:(t + 1) * LANES]
    o_ref[...] = ys_ref[...].astype(o_ref.dtype)


def _s5_scan_call(p_s5, m_in, m_out, a8, bsz, s):
    m = bsz * s
    j = s // S5_SUB
    kin = S5_SUB * LANES
    u_blk0 = D_BRANCH // LANES
    return pl.pallas_call(
        _s5_scan_kernel,
        grid=(S5_LANE_BLOCKS, bsz),
        in_specs=[pl.BlockSpec((s, LANES), lambda lb, b: (b, u_blk0 + lb)),
                  pl.BlockSpec((1, kin, kin), lambda lb, b: (lb, 0, 0)),
                  pl.BlockSpec((1, 2 * kin, kin), lambda lb, b: (lb, 0, 0)),
                  pl.BlockSpec((1, 1, kin), lambda lb, b: (lb, 0, 0))],
        out_specs=pl.BlockSpec((s, LANES), lambda lb, b: (b, lb)),
        out_shape=jax.ShapeDtypeStruct((m, D_BRANCH), BF16),
        scratch_shapes=[pltpu.VMEM((s, LANES), F32), pltpu.VMEM((j, kin), F32),
                        pltpu.VMEM((j, kin), F32), pltpu.VMEM((s, LANES), F32)],
        compiler_params=_cp(("parallel", "parallel")),
        name="s5_scan",
    )(p_s5, m_in, m_out, a8)


def _s5_post_kernel(p_ref, y_ref, d_ref, wg_ref, bg_ref, nw_ref, o_ref):
    d = D_BRANCH
    z = p_ref[:, 0:d].astype(F32)
    u = p_ref[:, d:2 * d].astype(F32)
    y = y_ref[...].astype(F32) + d_ref[...] * u
    y = 0.5 * y * (1.0 + jnp.tanh(math.sqrt(2.0 / math.pi) * (y + 0.044715 * (y * y * y))))
    g = _dot(y.astype(BF16), wg_ref[...]) + bg_ref[...]
    y = g[:, 0:d] * _sigmoid(g[:, d:2 * d])
    ms = jnp.mean(y * y, axis=-1, keepdims=True)
    o_ref[...] = (y * lax.rsqrt(ms + NORM_EPS) * nw_ref[...] * _silu(z)).astype(o_ref.dtype)


def _s5_post_call(p_s5, y_ssm, d_skip, w_glu, b_glu, norm_w):
    m = p_s5.shape[0]
    tm = min(512, m)
    const = lambda shape: pl.BlockSpec(shape, lambda i: (0,) * len(shape))
    return pl.pallas_call(
        _s5_post_kernel,
        grid=(m // tm,),
        in_specs=[pl.BlockSpec((tm, _W_S5), lambda i: (i, 0)),
                  pl.BlockSpec((tm, D_BRANCH), lambda i: (i, 0)),
                  const((1, D_BRANCH)), const((D_BRANCH, 2 * D_BRANCH)),
                  const((1, 2 * D_BRANCH)), const((1, D_BRANCH))],
        out_specs=pl.BlockSpec((tm, D_BRANCH), lambda i: (i, 0)),
        out_shape=jax.ShapeDtypeStruct((m, D_BRANCH), BF16),
        compiler_params=_cp(("parallel",)),
        name="s5_post",
    )(p_s5, y_ssm, d_skip.reshape(1, D_BRANCH), w_glu.astype(BF16),
      b_glu.reshape(1, 2 * D_BRANCH), norm_w.reshape(1, D_BRANCH))


def _outproj_kernel(y1, y2, y3, y4, w_ref, h_ref, g_ref, fw_ref, o_ref, *, final_norm):
    acc = _dot(y1[...], w_ref[0 * D_BRANCH:1 * D_BRANCH, :])
    acc = acc + _dot(y2[...], w_ref[1 * D_BRANCH:2 * D_BRANCH, :])
    acc = acc + _dot(y3[...], w_ref[2 * D_BRANCH:3 * D_BRANCH, :])
    acc = acc + _dot(y4[...], w_ref[3 * D_BRANCH:4 * D_BRANCH, :])
    h = h_ref[...] + g_ref[0] * acc
    if final_norm:
        ms = jnp.mean(h * h, axis=-1, keepdims=True)
        h = h * lax.rsqrt(ms + NORM_EPS) * fw_ref[...]
    o_ref[...] = h


def _outproj_call(ys, w_out, h, gate, final_w, s, final_norm):
    m = h.shape[0]
    tm = min(512, s)
    per_batch = s // tm
    yspec = pl.BlockSpec((tm, D_BRANCH), lambda i: (i, 0))
    return pl.pallas_call(
        functools.partial(_outproj_kernel, final_norm=final_norm),
        grid=(m // tm,),
        in_specs=[yspec] * 4 + [
            pl.BlockSpec((4 * D_BRANCH, D_MODEL), lambda i: (0, 0)),
            pl.BlockSpec((tm, D_MODEL), lambda i: (i, 0)),
            pl.BlockSpec((1, 1, D_MODEL), lambda i: (i // per_batch, 0, 0)),
            pl.BlockSpec((1, D_MODEL), lambda i: (0, 0)),
        ],
        out_specs=pl.BlockSpec((tm, D_MODEL), lambda i: (i, 0)),
        out_shape=jax.ShapeDtypeStruct((m, D_MODEL), F32),
        compiler_params=_cp(("parallel",)),
        name="out_proj",
    )(*ys, w_out, h, gate, final_w)


def _pack_w_in(w_in):
    w_ssd = w_in[..., _O_SSD_Z:_O_SSD_DT]
    w_ml = w_in[..., _O_ML_Z:_O_ML_I]
    w_s5 = w_in[..., _O_S5_Z:_O_RET_Z]
    w_ret = w_in[..., _O_RET_Z:_D_IN]
    pad = lambda a: jnp.pad(a, [(0, 0)] * (a.ndim - 1) + [(0, LANES - a.shape[-1])])
    w_sm = jnp.concatenate([pad(w_in[..., _O_SSD_DT:_O_ML_Z]), pad(w_in[..., _O_ML_I:_O_ML_F]),
                            pad(w_in[..., _O_ML_F:_O_S5_Z])], axis=-1)
    return tuple(w.astype(BF16) for w in (w_ssd, w_ml, w_s5, w_ret, w_sm))


def kernel(x, c, positions, norm_w, w_ada, b_ada, w_in, w_out, ssd_conv_w, ssd_conv_b, ssd_dt_bias, ssd_a_log, ssd_d, ssd_norm_w, ml_conv_w, ml_conv_b, ml_i_bias, ml_f_bias, ml_norm_w, s5_lambda_re, s5_lambda_im, s5_b_re, s5_b_im, s5_c_re, s5_c_im, s5_d, s5_log_step, s5_w_glu, s5_b_glu, s5_norm_w, ret_norm_w, final_norm_w):
    bsz, s, d = x.shape
    depth = w_in.shape[0]
    assert d == D_MODEL and s % CHUNK == 0 and s % S5_SUB == 0
    m = bsz * s

    mod = _mod_call(c.astype(F32), w_ada, b_ada)
    cos_t, sin_t = _rope_call(positions)
    packed = _pack_w_in(w_in)
    w_out_bf = w_out.astype(BF16)
    final_w = final_norm_w.reshape(1, D_MODEL).astype(F32)

    h = x.reshape(m, D_MODEL)
    for l in range(depth):
        shift = mod[l, :, 0:D_MODEL].reshape(bsz, 1, D_MODEL)
        scale = mod[l, :, D_MODEL:2 * D_MODEL].reshape(bsz, 1, D_MODEL)
        gate = mod[l, :, 2 * D_MODEL:].reshape(bsz, 1, D_MODEL)
        p_ssd, p_ml, p_s5, p_ret, p_sm = _inproj_call(
            h, norm_w[l].reshape(1, D_MODEL), scale, shift, [w[l] for w in packed], s)
        y_ssd = _ssd_call(p_ssd, p_sm, ssd_conv_w[l], ssd_conv_b[l], ssd_dt_bias[l], ssd_a_log[l],
                          ssd_d[l], ssd_norm_w[l], bsz, s)
        y_ml = _mlstm_call(p_ml, p_sm, ml_conv_w[l], ml_conv_b[l], ml_i_bias[l], ml_f_bias[l],
                           ml_norm_w[l], bsz, s)
        m_in, m_out, a8 = _s5_matrices(s5_lambda_re[l], s5_lambda_im[l], s5_b_re[l], s5_b_im[l],
                                       s5_c_re[l], s5_c_im[l], s5_log_step[l])
        y_ssm = _s5_scan_call(p_s5, m_in, m_out, a8, bsz, s)
        y_s5 = _s5_post_call(p_s5, y_ssm, s5_d[l], s5_w_glu[l], s5_b_glu[l], s5_norm_w[l])
        y_ret = _ret_call(p_ret, cos_t, sin_t, ret_norm_w[l], bsz, s)
        h = _outproj_call((y_ssd, y_ml, y_s5, y_ret), w_out_bf[l], h, gate, final_w, s,
                          final_norm=(l == depth - 1))
    return h.reshape(bsz, s, D_MODEL)
```

```python
import functools
import math

import numpy as np
import jax
import jax.numpy as jnp
from jax import lax
from jax.experimental import pallas as pl
from jax.experimental.pallas import tpu as pltpu

F32 = jnp.float32
BF16 = jnp.bfloat16

D_MODEL = 1024
D_BRANCH = 512
CHUNK = 128
CONV_K = 4
CONV_PAD = 8
NORM_EPS = 1e-6
NEG_BIG = -1e30

SSD_HEADS = 8
SSD_HEAD_DIM = 64
SSD_GROUPS = 2
SSD_STATE = 128
ML_HEADS = 4
ML_HEAD_DIM = 128
S5_GROUPS = 32
S5_GROUP = 16
S5_STATE = 64
S5_SUB = 8
S5_LANE_BLOCKS = 4
S5_LOCAL_GROUPS = S5_GROUPS // S5_LANE_BLOCKS
RET_HEADS = 4
RET_QK = 64
RET_V = 128
RET_DECAY_BASE = 5.0
ROPE_BASE = 10000.0
LANES = 128

_O_SSD_Z, _O_SSD_DT = 0, 1536
_O_ML_Z, _O_ML_I, _O_ML_F = 1544, 4104, 4108
_O_S5_Z = 4112
_O_RET_Z = 5136
_D_IN = 6672
_W_SSD, _W_ML, _W_S5, _W_RET, _W_SM = 1536, 2560, 1024, 1536, 384
_W_ALL = _W_SSD + _W_ML + _W_S5 + _W_RET + _W_SM

VMEM_LIMIT = 56 * 1024 * 1024
ROW_TILE = 512


def _cp(sem):
    return pltpu.CompilerParams(dimension_semantics=sem, vmem_limit_bytes=VMEM_LIMIT)


def _layer_spec(l, shape, nargs):
    return pl.BlockSpec((None,) + tuple(shape), lambda *_: (l,) + (0,) * len(shape))


def _dot(a, b):
    return jnp.dot(a, b, preferred_element_type=F32)


def _dot_nt(a, b):
    return lax.dot_general(a, b, (((1,), (1,)), ((), ())), preferred_element_type=F32)


def _dot_tn(a, b):
    return lax.dot_general(a, b, (((0,), (0,)), ((), ())), preferred_element_type=F32)


def _sigmoid(x):
    return 0.5 + 0.5 * jnp.tanh(0.5 * x)


def _silu(x):
    hx = 0.5 * x
    return hx + hx * jnp.tanh(hx)


def _softplus(x):
    return jnp.maximum(x, 0.0) + jnp.log1p(jnp.exp(-jnp.abs(x)))


def _split3(v):
    hi = v.astype(BF16)
    r1 = v - hi.astype(F32)
    mid = r1.astype(BF16)
    lo = (r1 - mid.astype(F32)).astype(BF16)
    return hi, mid, lo


def _dot01_l(m01, v):
    hi, mid, lo = _split3(v)
    return _dot(m01, hi) + _dot(m01, mid) + _dot(m01, lo)


def _dot01_r(v, m01):
    hi, mid, lo = _split3(v)
    return _dot(hi, m01) + _dot(mid, m01) + _dot(lo, m01)


def _tri_mask():
    row = lax.broadcasted_iota(jnp.int32, (CHUNK, CHUNK), 0)
    col = lax.broadcasted_iota(jnp.int32, (CHUNK, CHUNK), 1)
    return row >= col


def _rms(y):
    return y * lax.rsqrt(jnp.mean(y * y, axis=-1, keepdims=True) + NORM_EPS)


def _mod_kernel(c_ref, w_ref, b_ref, o_ref):
    cond = _silu(c_ref[...])
    o_ref[0] = _dot(cond.astype(BF16), w_ref[0].astype(BF16)) + b_ref[0]


def _mod_call(c, w_ada, b_ada):
    depth, d, n = w_ada.shape
    bsz = c.shape[0]
    tn = 1024
    return pl.pallas_call(
        _mod_kernel,
        grid=(depth, n // tn),
        in_specs=[
            pl.BlockSpec((bsz, d), lambda l, j: (0, 0)),
            pl.BlockSpec((1, d, tn), lambda l, j: (l, 0, j)),
            pl.BlockSpec((1, 1, tn), lambda l, j: (l, 0, j)),
        ],
        out_specs=pl.BlockSpec((1, bsz, tn), lambda l, j: (l, 0, j)),
        out_shape=jax.ShapeDtypeStruct((depth, bsz, n), F32),
        compiler_params=_cp(("parallel", "parallel")),
        name="adaln_mod",
    )(c, w_ada, b_ada.reshape(depth, 1, n))


def _rope_kernel(pos_ref, f_ref, sg_ref, cos_ref, sin_ref):
    ang = pos_ref[...] * f_ref[...]
    cos_ref[...] = jnp.cos(ang)
    sin_ref[...] = jnp.sin(ang) * sg_ref[...]


def _rope_call(positions):
    bsz, s = positions.shape
    m = bsz * s
    half = RET_QK // 2
    inv_freq = jnp.exp(-math.log(ROPE_BASE) * jnp.arange(half, dtype=F32) / half)
    f128 = jnp.tile(inv_freq, LANES // half).reshape(1, LANES)
    sign = np.tile(np.concatenate([-np.ones(half), np.ones(half)]), LANES // RET_QK)
    sign = jnp.asarray(sign.reshape(1, LANES), F32)
    pos = jnp.broadcast_to(positions.astype(F32).reshape(m, 1), (m, LANES))
    tm = min(1024, m)
    return pl.pallas_call(
        _rope_kernel,
        grid=(m // tm,),
        in_specs=[
            pl.BlockSpec((tm, LANES), lambda i: (i, 0)),
            pl.BlockSpec((1, LANES), lambda i: (0, 0)),
            pl.BlockSpec((1, LANES), lambda i: (0, 0)),
        ],
        out_specs=[pl.BlockSpec((tm, LANES), lambda i: (i, 0))] * 2,
        out_shape=[jax.ShapeDtypeStruct((m, LANES), F32)] * 2,
        compiler_params=_cp(("parallel",)),
        name="rope_tables",
    )(pos, f128, sign)


def _conv_silu(x, buf_ref, cw_ref, cb_ref):
    n = x.shape[0]
    buf_ref[CONV_PAD:CONV_PAD + n, :] = x
    acc = cb_ref[...] + cw_ref[CONV_K - 1:CONV_K, :] * x
    for k in range(CONV_K - 1):
        off = CONV_PAD - (CONV_K - 1) + k
        acc = acc + cw_ref[k:k + 1, :] * buf_ref[off:off + n, :]
    buf_ref[0:CONV_PAD, :] = x[n - CONV_PAD:n, :]
    return _silu(acc)


def _inproj_kernel(h_ref, nw_ref, sc_ref, sh_ref, w_ref, scw_ref, scb_ref, mcw_ref, mcb_ref,
                   o_ssd, o_ml, o_s5, o_ret, o_sm, sbuf_ref, mbuf_ref, *, per_batch):
    @pl.when(pl.program_id(0) % per_batch == 0)
    def _():
        sbuf_ref[0:CONV_PAD, :] = jnp.zeros((CONV_PAD, sbuf_ref.shape[1]), F32)
        mbuf_ref[0:CONV_PAD, :] = jnp.zeros((CONV_PAD, mbuf_ref.shape[1]), F32)

    x = h_ref[...]
    hn = _rms(x) * nw_ref[...]
    hb = (hn * (1.0 + sc_ref[...]) + sh_ref[...]).astype(BF16)
    d = D_BRANCH

    def proj(col, width):
        return _dot(hb, w_ref[:, col:col + width])

    col = 0
    o_ssd[:, 0:d] = _silu(proj(col, d)).astype(BF16)
    o_ssd[:, d:3 * d] = _conv_silu(proj(col + d, 2 * d), sbuf_ref, scw_ref, scb_ref).astype(BF16)
    col += _W_SSD
    o_ml[:, 0:d] = _silu(proj(col, d)).astype(BF16)
    qk = _conv_silu(proj(col + d, 2 * d), mbuf_ref, mcw_ref, mcb_ref)
    o_ml[:, d:2 * d] = qk[:, 0:d].astype(BF16)
    o_ml[:, 2 * d:3 * d] = (qk[:, d:2 * d] * (ML_HEAD_DIM ** -0.5)).astype(BF16)
    o_ml[:, 3 * d:4 * d] = proj(col + 3 * d, d).astype(BF16)
    o_ml[:, 4 * d:5 * d] = _sigmoid(proj(col + 4 * d, d)).astype(BF16)
    col += _W_ML
    o_s5[:, 0:d] = _silu(proj(col, d)).astype(BF16)
    o_s5[:, d:2 * d] = proj(col + d, d).astype(BF16)
    col += _W_S5
    o_ret[:, 0:d] = _silu(proj(col, d)).astype(BF16)
    o_ret[:, d:3 * d] = proj(col + d, 2 * d).astype(BF16)
    col += _W_RET
    o_sm[...] = proj(col, _W_SM)


def _inproj_call(l, h, norm_w, mod5, w_all, ssd_cw, ssd_cb, ml_cw, ml_cb, s):
    m = h.shape[0]
    tm = min(ROW_TILE, s)
    per_batch = s // tm
    widths = (_W_SSD, _W_ML, _W_S5, _W_RET, _W_SM)
    dtypes = (BF16, BF16, BF16, BF16, F32)
    cdim = 2 * D_BRANCH
    mod_spec = lambda which: pl.BlockSpec((None, None, None, 1, D_MODEL),
                                          lambda i: (l, i // per_batch, which, 0, 0))
    return pl.pallas_call(
        functools.partial(_inproj_kernel, per_batch=per_batch),
        grid=(m // tm,),
        in_specs=[
            pl.BlockSpec((tm, D_MODEL), lambda i: (i, 0)),
            _layer_spec(l, (1, D_MODEL), 1),
            mod_spec(1), mod_spec(0),
            pl.BlockSpec((None, D_MODEL, _W_ALL), lambda i: (l, 0, 0), pipeline_mode=pl.Buffered(1)),
            _layer_spec(l, (CONV_K, cdim), 1), _layer_spec(l, (1, cdim), 1),
            _layer_spec(l, (CONV_K, cdim), 1), _layer_spec(l, (1, cdim), 1),
        ],
        out_specs=[pl.BlockSpec((tm, n), lambda i: (i, 0)) for n in widths],
        out_shape=[jax.ShapeDtypeStruct((m, n), dt) for n, dt in zip(widths, dtypes)],
        scratch_shapes=[pltpu.VMEM((tm + CONV_PAD, cdim), F32), pltpu.VMEM((tm + CONV_PAD, cdim), F32)],
        compiler_params=_cp(("arbitrary",)),
        name="in_proj",
    )(h, norm_w, mod5, mod5, w_all, ssd_cw, ssd_cb, ml_cw, ml_cb)


def _ssd_kernel(p_ref, sm_ref, dtb_ref, alog_ref, dsk_ref, nw_ref, e8_ref, o_ref, st_ref):
    @pl.when(pl.program_id(1) == 0)
    def _():
        st_ref[...] = jnp.zeros(st_ref.shape, F32)

    tri = _tri_mask()
    tri_bf = jnp.where(tri, 1.0, 0.0).astype(BF16)
    e8 = e8_ref[...]

    gate = p_ref[:, 0:D_BRANCH].astype(F32)
    xs = p_ref[:, D_BRANCH:2 * D_BRANCH].astype(F32)
    gw = SSD_GROUPS * SSD_STATE
    bm = p_ref[:, 2 * D_BRANCH:2 * D_BRANCH + gw]
    cm = p_ref[:, 2 * D_BRANCH + gw:2 * D_BRANCH + 2 * gw]

    dt = _softplus(sm_ref[...] + dtb_ref[...])
    da = dt * (-jnp.exp(alog_ref[...]))
    a_cum = _dot01_l(tri_bf, da)
    a_cum_t = a_cum.T
    ac_e = _dot01_r(a_cum, e8)
    dt_e = _dot01_r(dt, e8)
    a_last_e = ac_e[CHUNK - 1:CHUNK, :]
    xd = xs * dt_e
    xd_bf = xd.astype(BF16)
    xw_bf = (xd * jnp.exp(a_last_e - ac_e)).astype(BF16)
    ea_e = jnp.exp(ac_e)

    hw = (SSD_HEADS // SSD_GROUPS) * SSD_HEAD_DIM
    lane = lax.broadcasted_iota(jnp.int32, (CHUNK, hw), 1)
    ys = []
    for g in range(SSD_GROUPS):
        bg = bm[:, g * SSD_STATE:(g + 1) * SSD_STATE]
        cg = cm[:, g * SSD_STATE:(g + 1) * SSD_STATE]
        cb = _dot_nt(cg, bg)
        s_prev = st_ref[g]
        y = _dot(cg, s_prev.astype(BF16)) * ea_e[:, g * hw:(g + 1) * hw]
        xd_g = xd_bf[:, g * hw:(g + 1) * hw]
        for hh in range(SSD_HEADS // SSD_GROUPS):
            h = g * (SSD_HEADS // SSD_GROUPS) + hh
            seg = a_cum[:, h:h + 1] - a_cum_t[h:h + 1, :]
            dec = jnp.exp(jnp.where(tri, seg, NEG_BIG))
            m_h = (cb * dec).astype(BF16)
            in_head = (lane >= hh * SSD_HEAD_DIM) & (lane < (hh + 1) * SSD_HEAD_DIM)
            y = y + _dot(m_h, jnp.where(in_head, xd_g, jnp.zeros_like(xd_g)))
        ys.append(y)
        s_loc = _dot_tn(bg, xw_bf[:, g * hw:(g + 1) * hw])
        st_ref[g] = s_prev * jnp.exp(a_last_e[:, g * hw:(g + 1) * hw]) + s_loc

    y = jnp.concatenate(ys, axis=1) + dsk_ref[...] * xs
    o_ref[...] = (_rms(y * gate) * nw_ref[...]).astype(o_ref.dtype)


def _ssd_call(l, p_ssd, p_sm, dt_bias, a_log, d_skip, norm_w, bsz, s):
    nc = s // CHUNK
    m = bsz * s
    e8 = np.zeros((LANES, D_BRANCH), np.float32)
    for h in range(SSD_HEADS):
        e8[h, h * SSD_HEAD_DIM:(h + 1) * SSD_HEAD_DIM] = 1.0
    row = lambda n: pl.BlockSpec((CHUNK, n), lambda b, c: (b * nc + c, 0))
    return pl.pallas_call(
        _ssd_kernel,
        grid=(bsz, nc),
        in_specs=[row(_W_SSD), row(LANES),
                  _layer_spec(l, (1, LANES), 2), _layer_spec(l, (1, LANES), 2),
                  _layer_spec(l, (1, D_BRANCH), 2), _layer_spec(l, (1, D_BRANCH), 2),
                  pl.BlockSpec((LANES, D_BRANCH), lambda b, c: (0, 0))],
        out_specs=row(D_BRANCH),
        out_shape=jax.ShapeDtypeStruct((m, D_BRANCH), BF16),
        scratch_shapes=[pltpu.VMEM((SSD_GROUPS, SSD_STATE, 256), F32)],
        compiler_params=_cp(("parallel", "arbitrary")),
        name="ssd_mixer",
    )(p_ssd, p_sm, dt_bias, a_log, d_skip, norm_w, jnp.asarray(e8, BF16))


def _mlstm_kernel(p_ref, sm_ref, ib_ref, fb_ref, nw_ref, o_ref, c_ref, m_ref):
    @pl.when(pl.program_id(1) == 0)
    def _():
        c_ref[...] = jnp.zeros(c_ref.shape, F32)
        m_ref[...] = jnp.zeros(m_ref.shape, F32)

    tri = _tri_mask()
    tri_bf = jnp.where(tri, 1.0, 0.0).astype(BF16)
    d = D_BRANCH

    gate = p_ref[:, 0:d].astype(F32)
    q = p_ref[:, d:2 * d]
    k = p_ref[:, 2 * d:3 * d]
    v = p_ref[:, 3 * d:4 * d]
    o_gate = p_ref[:, 4 * d:5 * d].astype(F32)

    i_log = sm_ref[:, LANES:2 * LANES] + ib_ref[...]
    f_log = -_softplus(-(sm_ref[:, 2 * LANES:3 * LANES] + fb_ref[...]))
    f_cum = _dot01_l(tri_bf, f_log)
    r = i_log - f_cum
    r_t = r.T
    f_last = f_cum[CHUNK - 1:CHUNK, :]
    m_prev = m_ref[...]
    a_loc = f_last + r
    m_new = jnp.maximum(f_last + m_prev, jnp.max(a_loc, axis=0, keepdims=True))
    w_loc = jnp.exp(a_loc - m_new)
    s_old = jnp.exp(f_last + m_prev - m_new)
    m_ref[...] = m_new

    ones = jnp.ones((CHUNK, ML_HEAD_DIM), BF16)
    outs = []
    for h in range(ML_HEADS):
        sl = slice(h * ML_HEAD_DIM, (h + 1) * ML_HEAD_DIM)
        qh = q[:, sl]
        kh = k[:, sl]
        v_ext = jnp.concatenate([v[:, sl], ones], axis=1)
        g_col = f_cum[:, h:h + 1]
        dmat = jnp.where(tri, g_col + r_t[h:h + 1, :], NEG_BIG)
        g_inter = g_col + m_prev[:, h:h + 1]
        m_t = jnp.maximum(g_inter, jnp.max(dmat, axis=-1, keepdims=True))
        sm = (_dot_nt(qh, kh) * jnp.exp(dmat - m_t)).astype(BF16)
        c_prev = c_ref[h]
        numden = _dot(sm, v_ext) + jnp.exp(g_inter - m_t) * _dot(qh, c_prev.astype(BF16))
        num = numden[:, 0:ML_HEAD_DIM]
        den = numden[:, ML_HEAD_DIM:]
        hh = num / jnp.maximum(jnp.abs(den), jnp.exp(-m_t))
        outs.append(_rms(o_gate[:, sl] * hh))
        kw = (kh.astype(F32) * w_loc[:, h:h + 1]).astype(BF16)
        c_ref[h] = s_old[:, h:h + 1] * c_prev + _dot_tn(kw, v_ext)

    y = jnp.concatenate(outs, axis=1) * nw_ref[...] * gate
    o_ref[...] = y.astype(o_ref.dtype)


def _mlstm_call(l, p_ml, p_sm, i_bias, f_bias, norm_w, bsz, s):
    nc = s // CHUNK
    m = bsz * s
    row = lambda n: pl.BlockSpec((CHUNK, n), lambda b, c: (b * nc + c, 0))
    return pl.pallas_call(
        _mlstm_kernel,
        grid=(bsz, nc),
        in_specs=[row(_W_ML), row(_W_SM),
                  _layer_spec(l, (1, LANES), 2), _layer_spec(l, (1, LANES), 2),
                  _layer_spec(l, (1, D_BRANCH), 2)],
        out_specs=row(D_BRANCH),
        out_shape=jax.ShapeDtypeStruct((m, D_BRANCH), BF16),
        scratch_shapes=[pltpu.VMEM((ML_HEADS, ML_HEAD_DIM, 2 * ML_HEAD_DIM), F32),
                        pltpu.VMEM((1, LANES), F32)],
        compiler_params=_cp(("parallel", "arbitrary")),
        name="mlstm_mixer",
    )(p_ml, p_sm, i_bias, f_bias, norm_w)


def _ret_log_gamma():
    return [math.log1p(-2.0 ** (-(RET_DECAY_BASE + h))) for h in range(RET_HEADS)]


def _ret_consts():
    lg = _ret_log_gamma()
    pos = np.arange(CHUNK, dtype=np.float64)
    rel = pos[:, None] - pos[None, :]
    dm = np.stack([np.where(rel >= 0, np.exp(np.maximum(rel, 0.0) * g), 0.0) for g in lg])
    from_start = np.concatenate(
        [np.repeat(np.exp((pos + 1.0) * g)[:, None], RET_V, axis=1) for g in lg], axis=1)
    to_end = np.concatenate(
        [np.repeat(np.exp((CHUNK - 1.0 - pos) * g)[:, None], RET_QK, axis=1) for g in lg], axis=1)
    qw, vw = RET_HEADS * RET_QK, RET_HEADS * RET_V
    chunk_decay = np.zeros((qw, vw))
    for h, g in enumerate(lg):
        chunk_decay[h * RET_QK:(h + 1) * RET_QK, h * RET_V:(h + 1) * RET_V] = math.exp(CHUNK * g)
    block_diag = (chunk_decay > 0).astype(np.float64)
    swap = np.zeros((qw, qw))
    for j in range(qw):
        base, off = (j // RET_QK) * RET_QK, j % RET_QK
        swap[base + (off + RET_QK // 2) % RET_QK, j] = 1.0
    f = lambda a: jnp.asarray(a, F32)
    return f(dm), f(from_start), f(to_end), f(chunk_decay), f(block_diag), jnp.asarray(swap, BF16)


def _ret_kernel(p_ref, cos_ref, sin_ref, dm_ref, fs_ref, te_ref, cd_ref, bd_ref, sw_ref, nw_ref,
                o_ref, r_ref):
    @pl.when(pl.program_id(1) == 0)
    def _():
        r_ref[...] = jnp.zeros(r_ref.shape, F32)

    d = D_BRANCH
    qw = RET_HEADS * RET_QK
    gate = p_ref[:, 0:d].astype(F32)
    q_raw = p_ref[:, d:d + qw]
    k_raw = p_ref[:, d + qw:d + 2 * qw]
    v = p_ref[:, d + 2 * qw:2 * d + 2 * qw]
    cos_e = jnp.concatenate([cos_ref[...]] * (qw // LANES), axis=1)
    sin_e = jnp.concatenate([sin_ref[...]] * (qw // LANES), axis=1)
    sw = sw_ref[...]
    q = q_raw.astype(F32) * cos_e + _dot(q_raw, sw) * sin_e
    k = (k_raw.astype(F32) * cos_e + _dot(k_raw, sw) * sin_e) * (RET_QK ** -0.5)
    k_bf = k.astype(BF16)

    r_prev = r_ref[...]
    y = _dot(q.astype(BF16), r_prev.astype(BF16)) * fs_ref[...]
    lane = lax.broadcasted_iota(jnp.int32, (CHUNK, qw), 1)
    inner = []
    for h in range(RET_HEADS):
        in_head = (lane >= h * RET_QK) & (lane < (h + 1) * RET_QK)
        qh = jnp.where(in_head, q, jnp.zeros_like(q)).astype(BF16)
        scores = (_dot_nt(qh, k_bf) * dm_ref[h]).astype(BF16)
        inner.append(_dot(scores, v[:, h * RET_V:(h + 1) * RET_V]))
    y = y + jnp.concatenate(inner, axis=1)
    r_ref[...] = r_prev * cd_ref[...] + _dot_tn((k * te_ref[...]).astype(BF16), v) * bd_ref[...]

    outs = [_rms(y[:, h * RET_V:(h + 1) * RET_V]) for h in range(RET_HEADS)]
    o_ref[...] = (jnp.concatenate(outs, axis=1) * nw_ref[...] * gate).astype(o_ref.dtype)


def _ret_call(l, p_ret, cos_t, sin_t, consts, norm_w, bsz, s):
    nc = s // CHUNK
    m = bsz * s
    qw, vw = RET_HEADS * RET_QK, RET_HEADS * RET_V
    dm, fs, te, cd, bd, sw = consts
    const = lambda shape: pl.BlockSpec(shape, lambda b, c: (0,) * len(shape))
    row = lambda n: pl.BlockSpec((CHUNK, n), lambda b, c: (b * nc + c, 0))
    return pl.pallas_call(
        _ret_kernel,
        grid=(bsz, nc),
        in_specs=[row(_W_RET), row(LANES), row(LANES),
                  const((RET_HEADS, CHUNK, CHUNK)), const((CHUNK, vw)), const((CHUNK, qw)),
                  const((qw, vw)), const((qw, vw)), const((qw, qw)),
                  _layer_spec(l, (1, D_BRANCH), 2)],
        out_specs=row(D_BRANCH),
        out_shape=jax.ShapeDtypeStruct((m, D_BRANCH), BF16),
        scratch_shapes=[pltpu.VMEM((qw, vw), F32)],
        compiler_params=_cp(("parallel", "arbitrary")),
        name="retention_mixer",
    )(p_ret, cos_t, sin_t, dm, fs, te, cd, bd, sw, norm_w)


def _s5_compact(lam_re, lam_im, b_re, b_im, c_re, c_im, log_step):
    depth = lam_re.shape[0]
    nb, ngl, sub = S5_LANE_BLOCKS, S5_LOCAL_GROUPS, S5_SUB
    step = jnp.exp(log_step.astype(F32))[..., None]
    lr = jnp.minimum(lam_re.astype(F32), -1e-4)
    li = lam_im.astype(F32)
    mag = jnp.exp(lr * step)
    ang = li * step
    ab_re = mag * jnp.cos(ang)
    ab_im = mag * jnp.sin(ang)
    den = lr * lr + li * li
    coef_re = ((ab_re - 1.0) * lr + ab_im * li) / den
    coef_im = (ab_im * lr - (ab_re - 1.0) * li) / den
    br, bi = b_re.astype(F32), b_im.astype(F32)
    bb_re = coef_re[..., None] * br - coef_im[..., None] * bi
    bb_im = coef_re[..., None] * bi + coef_im[..., None] * br
    pw_re, pw_im = [jnp.ones_like(ab_re)], [jnp.zeros_like(ab_im)]
    for _ in range(sub):
        pr, pi = pw_re[-1], pw_im[-1]
        pw_re.append(pr * ab_re - pi * ab_im)
        pw_im.append(pr * ab_im + pi * ab_re)
    pw_re, pw_im = jnp.stack(pw_re), jnp.stack(pw_im)
    cr, ci = c_re.astype(F32), c_im.astype(F32)
    hp = lax.Precision.HIGHEST
    ab_b_re = pw_re[..., None] * bb_re[None] - pw_im[..., None] * bb_im[None]
    ab_b_im = pw_re[..., None] * bb_im[None] + pw_im[..., None] * bb_re[None]
    ca_re = cr[None] * pw_re[:, :, :, None, :] - ci[None] * pw_im[:, :, :, None, :]
    ca_im = cr[None] * pw_im[:, :, :, None, :] + ci[None] * pw_re[:, :, :, None, :]
    taps = (jnp.einsum('lgop,klgpi->klgoi', cr, ab_b_re[:sub], precision=hp)
            - jnp.einsum('lgop,klgpi->klgoi', ci, ab_b_im[:sub], precision=hp))

    def rows_in(a):
        a = a[:sub][::-1].reshape(sub, depth, nb, ngl, S5_STATE, S5_GROUP)
        return a.transpose(1, 2, 0, 3, 5, 4).reshape(depth, nb, sub * ngl * S5_GROUP, S5_STATE)

    c_in = jnp.concatenate([rows_in(ab_b_re), rows_in(ab_b_im)], axis=-1)
    lag = np.arange(sub)[None, :] - np.arange(sub)[:, None]
    toe = taps[np.maximum(lag, 0)] * jnp.asarray(lag >= 0, F32)[:, :, None, None, None, None]
    toe = toe.reshape(sub, sub, depth, nb, ngl, S5_GROUP, S5_GROUP)
    toe = toe.transpose(2, 3, 0, 4, 6, 1, 5).reshape(depth, nb, sub * ngl * S5_GROUP, sub * S5_GROUP)

    def rows_x(a):
        a = a[1:].reshape(sub, depth, nb, ngl, S5_GROUP, S5_STATE)
        return a.transpose(1, 2, 3, 5, 0, 4).reshape(depth, nb, ngl * S5_STATE, sub * S5_GROUP)

    c_out = jnp.concatenate([toe, rows_x(ca_re), -rows_x(ca_im)], axis=2)
    a8 = jnp.concatenate([pw_re[sub].reshape(depth, nb, 1, ngl * S5_STATE),
                          pw_im[sub].reshape(depth, nb, 1, ngl * S5_STATE)], axis=-1)
    return c_in.astype(BF16), c_out.astype(BF16), a8


def _s5_expanders():
    kin = S5_SUB * LANES
    half = S5_LOCAL_GROUPS * S5_STATE
    sel_in = np.zeros((LANES, 2 * half), np.float32)
    for col in range(2 * half):
        sel_in[(col // half) * S5_STATE + col % S5_STATE, col] = 1.0
    sel_out = np.zeros((LANES, kin), np.float32)
    for col in range(kin):
        sel_out[(col // LANES) * S5_GROUP + col % S5_GROUP, col] = 1.0
    return jnp.asarray(sel_in, BF16), jnp.asarray(sel_out, BF16)


def _s5_scan_kernel(u_ref, cin_ref, cout_ref, a8_ref, sin_ref, sout_ref, o_ref,
                    uf_ref, xl_ref, xin_ref, ys_ref, min_ref, mout_ref, kin_ref, kout_ref):
    s = u_ref.shape[0]
    j = s // S5_SUB
    kin = S5_SUB * LANES
    half = S5_LOCAL_GROUPS * S5_STATE

    @pl.when((pl.program_id(0) == 0) & (pl.program_id(1) == 0))
    def _():
        sh_g, sh_p, gmask = S5_GROUP.bit_length() - 1, S5_STATE.bit_length() - 1, S5_LOCAL_GROUPS - 1
        r_in = lax.broadcasted_iota(jnp.int32, (kin, 2 * half), 0)
        c_in = lax.broadcasted_iota(jnp.int32, (kin, 2 * half), 1)
        same = ((r_in >> sh_g) & gmask) == ((c_in & (half - 1)) >> sh_p)
        kin_ref[...] = jnp.where(same, 1.0, 0.0).astype(BF16)
        r_o = lax.broadcasted_iota(jnp.int32, (2 * kin, kin), 0)
        c_o = lax.broadcasted_iota(jnp.int32, (2 * kin, kin), 1)
        g_row = jnp.where(r_o < kin, (r_o >> sh_g) & gmask, ((r_o - kin) & (half - 1)) >> sh_p)
        same = g_row == ((c_o >> sh_g) & gmask)
        kout_ref[...] = jnp.where(same, 1.0, 0.0).astype(BF16)

    @pl.when(pl.program_id(1) == 0)
    def _():
        min_ref[...] = _dot(cin_ref[...], sin_ref[...]).astype(BF16) * kin_ref[...]
        mout_ref[...] = _dot(cout_ref[...], sout_ref[...]).astype(BF16) * kout_ref[...]

    uf_ref[...] = u_ref[...].astype(F32)
    ub = jnp.concatenate([uf_ref[pl.ds(t, j, stride=S5_SUB), :] for t in range(S5_SUB)],
                         axis=1).astype(BF16)
    xl_ref[...] = _dot(ub, min_ref[...])
    a_re = a8_ref[:, 0:half]
    a_im = a8_ref[:, half:]

    def body(g, carry):
        x_re, x_im = carry
        r0 = pl.multiple_of(g * 8, 8)
        loc = xl_ref[pl.ds(r0, 8), :]
        rows = []
        for r in range(8):
            rows.append(jnp.concatenate([x_re, x_im], axis=1))
            n_re = a_re * x_re - a_im * x_im + loc[r:r + 1, 0:half]
            n_im = a_re * x_im + a_im * x_re + loc[r:r + 1, half:]
            x_re, x_im = n_re, n_im
        xin_ref[pl.ds(r0, 8), :] = jnp.concatenate(rows, axis=0)
        return x_re, x_im

    zero = jnp.zeros((1, half), F32)
    lax.fori_loop(0, j // 8, body, (zero, zero))
    lhs = jnp.concatenate([ub, xin_ref[...].astype(BF16)], axis=1)
    y = _dot(lhs, mout_ref[...])
    for t in range(S5_SUB):
        ys_ref[pl.ds(t, j, stride=S5_SUB), :] = y[:, t * LANES:(t + 1) * LANES]
    o_ref[...] = ys_ref[...].astype(o_ref.dtype)


def _s5_scan_call(l, p_s5, c_in, c_out, a8, sel_in, sel_out, bsz, s):
    m = bsz * s
    j = s // S5_SUB
    kin = S5_SUB * LANES
    u_blk0 = D_BRANCH // LANES
    lspec = lambda shape: pl.BlockSpec((None, None) + shape, lambda lb, b: (l, lb, 0, 0))
    const = lambda shape: pl.BlockSpec(shape, lambda lb, b: (0, 0))
    return pl.pallas_call(
        _s5_scan_kernel,
        grid=(S5_LANE_BLOCKS, bsz),
        in_specs=[pl.BlockSpec((s, LANES), lambda lb, b: (b, u_blk0 + lb)),
                  lspec((kin, LANES)), lspec((2 * kin, LANES)), lspec((1, kin)),
                  const((LANES, kin)), const((LANES, kin))],
        out_specs=pl.BlockSpec((s, LANES), lambda lb, b: (b, lb)),
        out_shape=jax.ShapeDtypeStruct((m, D_BRANCH), BF16),
        scratch_shapes=[pltpu.VMEM((s, LANES), F32), pltpu.VMEM((j, kin), F32),
                        pltpu.VMEM((j, kin), F32), pltpu.VMEM((s, LANES), F32),
                        pltpu.VMEM((kin, kin), BF16), pltpu.VMEM((2 * kin, kin), BF16),
                        pltpu.VMEM((kin, kin), BF16), pltpu.VMEM((2 * kin, kin), BF16)],
        compiler_params=_cp(("arbitrary", "arbitrary")),
        name="s5_scan",
    )(p_s5, c_in, c_out, a8, sel_in, sel_out)


def _s5_post_kernel(p_ref, y_ref, d_ref, wg_ref, bg_ref, nw_ref, o_ref):
    d = D_BRANCH
    gate = p_ref[:, 0:d].astype(F32)
    u = p_ref[:, d:2 * d].astype(F32)
    y = y_ref[...].astype(F32) + d_ref[...] * u
    y = 0.5 * y * (1.0 + jnp.tanh(math.sqrt(2.0 / math.pi) * (y + 0.044715 * (y * y * y))))
    g = _dot(y.astype(BF16), wg_ref[...]) + bg_ref[...]
    y = g[:, 0:d] * _sigmoid(g[:, d:2 * d])
    o_ref[...] = (_rms(y) * nw_ref[...] * gate).astype(o_ref.dtype)


def _s5_post_call(l, p_s5, y_ssm, d_skip, w_glu, b_glu, norm_w):
    m = p_s5.shape[0]
    tm = min(ROW_TILE, m)
    return pl.pallas_call(
        _s5_post_kernel,
        grid=(m // tm,),
        in_specs=[pl.BlockSpec((tm, _W_S5), lambda i: (i, 0)),
                  pl.BlockSpec((tm, D_BRANCH), lambda i: (i, 0)),
                  _layer_spec(l, (1, D_BRANCH), 1), _layer_spec(l, (D_BRANCH, 2 * D_BRANCH), 1),
                  _layer_spec(l, (1, 2 * D_BRANCH), 1), _layer_spec(l, (1, D_BRANCH), 1)],
        out_specs=pl.BlockSpec((tm, D_BRANCH), lambda i: (i, 0)),
        out_shape=jax.ShapeDtypeStruct((m, D_BRANCH), BF16),
        compiler_params=_cp(("parallel",)),
        name="s5_post",
    )(p_s5, y_ssm, d_skip, w_glu, b_glu, norm_w)


def _outproj_kernel(y1, y2, y3, y4, w_ref, h_ref, g_ref, fw_ref, o_ref, *, final_norm):
    acc = _dot(y1[...], w_ref[0 * D_BRANCH:1 * D_BRANCH, :])
    acc = acc + _dot(y2[...], w_ref[1 * D_BRANCH:2 * D_BRANCH, :])
    acc = acc + _dot(y3[...], w_ref[2 * D_BRANCH:3 * D_BRANCH, :])
    acc = acc + _dot(y4[...], w_ref[3 * D_BRANCH:4 * D_BRANCH, :])
    h = h_ref[...] + g_ref[...] * acc
    if final_norm:
        h = _rms(h) * fw_ref[...]
    o_ref[...] = h


def _outproj_call(l, ys, w_out, h, mod5, final_w, s, final_norm):
    m = h.shape[0]
    tm = min(ROW_TILE, s)
    per_batch = s // tm
    yspec = pl.BlockSpec((tm, D_BRANCH), lambda i: (i, 0))
    return pl.pallas_call(
        functools.partial(_outproj_kernel, final_norm=final_norm),
        grid=(m // tm,),
        in_specs=[yspec] * 4 + [
            _layer_spec(l, (4 * D_BRANCH, D_MODEL), 1),
            pl.BlockSpec((tm, D_MODEL), lambda i: (i, 0)),
            pl.BlockSpec((None, None, None, 1, D_MODEL), lambda i: (l, i // per_batch, 2, 0, 0)),
            pl.BlockSpec((1, D_MODEL), lambda i: (0, 0)),
        ],
        out_specs=pl.BlockSpec((tm, D_MODEL), lambda i: (i, 0)),
        out_shape=jax.ShapeDtypeStruct((m, D_MODEL), F32),
        compiler_params=_cp(("parallel",)),
        name="out_proj",
    )(*ys, w_out, h, mod5, final_w)


def _pack_w_in(w_in):
    pad = lambda a: jnp.pad(a, [(0, 0)] * (a.ndim - 1) + [(0, LANES - a.shape[-1])])
    parts = [w_in[..., _O_SSD_Z:_O_SSD_DT], w_in[..., _O_ML_Z:_O_ML_I], w_in[..., _O_S5_Z:_O_RET_Z],
             w_in[..., _O_RET_Z:_D_IN], pad(w_in[..., _O_SSD_DT:_O_ML_Z]),
             pad(w_in[..., _O_ML_I:_O_ML_F]), pad(w_in[..., _O_ML_F:_O_S5_Z])]
    return jnp.concatenate(parts, axis=-1).astype(BF16)


def _pad_lanes(v):
    return jnp.pad(v.astype(F32), ((0, 0), (0, LANES - v.shape[-1])))[:, None, :]


def kernel(x, c, positions, norm_w, w_ada, b_ada, w_in, w_out, ssd_conv_w, ssd_conv_b, ssd_dt_bias, ssd_a_log, ssd_d, ssd_norm_w, ml_conv_w, ml_conv_b, ml_i_bias, ml_f_bias, ml_norm_w, s5_lambda_re, s5_lambda_im, s5_b_re, s5_b_im, s5_c_re, s5_c_im, s5_d, s5_log_step, s5_w_glu, s5_b_glu, s5_norm_w, ret_norm_w, final_norm_w):
    bsz, s, d = x.shape
    depth = w_in.shape[0]
    assert d == D_MODEL and s % CHUNK == 0 and s % (8 * S5_SUB) == 0
    m = bsz * s
    row3 = lambda a: a.astype(F32)[:, None, :]

    mod = _mod_call(c.astype(F32), w_ada, b_ada)
    mod5 = mod.reshape(depth, bsz, 3, 1, D_MODEL)
    cos_t, sin_t = _rope_call(positions)
    ret_consts = _ret_consts()
    w_all = _pack_w_in(w_in)
    w_out_bf = w_out.astype(BF16)
    w_glu_bf = s5_w_glu.astype(BF16)
    c_in, c_out, a8 = _s5_compact(s5_lambda_re, s5_lambda_im, s5_b_re, s5_b_im, s5_c_re, s5_c_im,
                                  s5_log_step)
    sel_in, sel_out = _s5_expanders()
    norm_w3, final_w = row3(norm_w), final_norm_w.reshape(1, D_MODEL).astype(F32)
    ssd_cb3, ml_cb3 = row3(ssd_conv_b), row3(ml_conv_b)
    dt_bias3, a_log3 = _pad_lanes(ssd_dt_bias), _pad_lanes(ssd_a_log)
    d_skip3 = row3(jnp.repeat(ssd_d, SSD_HEAD_DIM, axis=-1))
    i_bias3, f_bias3 = _pad_lanes(ml_i_bias), _pad_lanes(ml_f_bias)
    ssd_nw3, ml_nw3, s5_nw3, ret_nw3 = row3(ssd_norm_w), row3(ml_norm_w), row3(s5_norm_w), row3(ret_norm_w)
    s5_d3, b_glu3 = row3(s5_d), row3(s5_b_glu)

    h = x.reshape(m, D_MODEL)
    for l in range(depth):
        p_ssd, p_ml, p_s5, p_ret, p_sm = _inproj_call(
            l, h, norm_w3, mod5, w_all, ssd_conv_w, ssd_cb3, ml_conv_w, ml_cb3, s)
        y_ssd = _ssd_call(l, p_ssd, p_sm, dt_bias3, a_log3, d_skip3, ssd_nw3, bsz, s)
        y_ml = _mlstm_call(l, p_ml, p_sm, i_bias3, f_bias3, ml_nw3, bsz, s)
        y_ssm = _s5_scan_call(l, p_s5, c_in, c_out, a8, sel_in, sel_out, bsz, s)
        y_s5 = _s5_post_call(l, p_s5, y_ssm, s5_d3, w_glu_bf, b_glu3, s5_nw3)
        y_ret = _ret_call(l, p_ret, cos_t, sin_t, ret_consts, ret_nw3, bsz, s)
        h = _outproj_call(l, (y_ssd, y_ml, y_s5, y_ret), w_out_bf, h, mod5, final_w, s,
                          final_norm=(l == depth - 1))
    return h.reshape(bsz, s, D_MODEL)
```

```python
import functools
import math

import numpy as np
import jax
import jax.numpy as jnp
from jax import lax
from jax.experimental import pallas as pl
from jax.experimental.pallas import tpu as pltpu

F32 = jnp.float32
BF16 = jnp.bfloat16

D_MODEL = 1024
D_BRANCH = 512
CHUNK = 128
CONV_K = 4
CONV_PAD = 8
NORM_EPS = 1e-6
NEG_BIG = -1e30

SSD_HEADS = 8
SSD_HEAD_DIM = 64
SSD_GROUPS = 2
SSD_STATE = 128
ML_HEADS = 4
ML_HEAD_DIM = 128
S5_GROUPS = 32
S5_GROUP = 16
S5_STATE = 64
S5_SUB = 8
S5_LANE_BLOCKS = 4
S5_LOCAL_GROUPS = S5_GROUPS // S5_LANE_BLOCKS
RET_HEADS = 4
RET_QK = 64
RET_V = 128
RET_DECAY_BASE = 5.0
ROPE_BASE = 10000.0
LANES = 128

_O_SSD_Z, _O_SSD_DT = 0, 1536
_O_ML_Z, _O_ML_I, _O_ML_F = 1544, 4104, 4108
_O_S5_Z = 4112
_O_RET_Z = 5136
_D_IN = 6672
_W_SSD, _W_ML, _W_S5, _W_RET, _W_SM = 1536, 2560, 1024, 1536, 384
_W_ALL = _W_SSD + _W_ML + _W_S5 + _W_RET + _W_SM
_W_S5G = _W_S5 - D_BRANCH

VMEM_LIMIT = 56 * 1024 * 1024
ROW_TILE = 512


def _cp(sem):
    return pltpu.CompilerParams(dimension_semantics=sem, vmem_limit_bytes=VMEM_LIMIT)


def _layer_spec(l, shape, nargs):
    return pl.BlockSpec((None,) + tuple(shape), lambda *_: (l,) + (0,) * len(shape))


def _dot(a, b):
    return jnp.dot(a, b, preferred_element_type=F32)


def _dot_nt(a, b):
    return lax.dot_general(a, b, (((1,), (1,)), ((), ())), preferred_element_type=F32)


def _dot_tn(a, b):
    return lax.dot_general(a, b, (((0,), (0,)), ((), ())), preferred_element_type=F32)


def _sigmoid(x):
    return 0.5 + 0.5 * jnp.tanh(0.5 * x)


def _silu(x):
    hx = 0.5 * x
    return hx + hx * jnp.tanh(hx)


def _softplus(x):
    return jnp.maximum(x, 0.0) + jnp.log1p(jnp.exp(-jnp.abs(x)))


def _split3(v):
    hi = v.astype(BF16)
    r1 = v - hi.astype(F32)
    mid = r1.astype(BF16)
    lo = (r1 - mid.astype(F32)).astype(BF16)
    return hi, mid, lo


def _dot01_l(m01, v):
    hi, mid, lo = _split3(v)
    return _dot(m01, hi) + _dot(m01, mid) + _dot(m01, lo)


def _dot01_r(v, m01):
    hi, mid, lo = _split3(v)
    return _dot(hi, m01) + _dot(mid, m01) + _dot(lo, m01)


def _tri_mask():
    row = lax.broadcasted_iota(jnp.int32, (CHUNK, CHUNK), 0)
    col = lax.broadcasted_iota(jnp.int32, (CHUNK, CHUNK), 1)
    return row >= col


def _rms(y):
    return y * lax.rsqrt(jnp.mean(y * y, axis=-1, keepdims=True) + NORM_EPS)


def _mod_kernel(c_ref, w_ref, b_ref, o_ref):
    cond = _silu(c_ref[...])
    o_ref[0] = _dot(cond.astype(BF16), w_ref[0].astype(BF16)) + b_ref[0]


def _mod_call(c, w_ada, b_ada):
    depth, d, n = w_ada.shape
    bsz = c.shape[0]
    tn = 1024
    return pl.pallas_call(
        _mod_kernel,
        grid=(depth, n // tn),
        in_specs=[
            pl.BlockSpec((bsz, d), lambda l, j: (0, 0)),
            pl.BlockSpec((1, d, tn), lambda l, j: (l, 0, j)),
            pl.BlockSpec((1, 1, tn), lambda l, j: (l, 0, j)),
        ],
        out_specs=pl.BlockSpec((1, bsz, tn), lambda l, j: (l, 0, j)),
        out_shape=jax.ShapeDtypeStruct((depth, bsz, n), F32),
        compiler_params=_cp(("parallel", "parallel")),
        name="adaln_mod",
    )(c, w_ada, b_ada.reshape(depth, 1, n))


def _rope_kernel(pos_ref, f_ref, sg_ref, cos_ref, sin_ref):
    ang = pos_ref[...] * f_ref[...]
    cos_ref[...] = jnp.cos(ang)
    sin_ref[...] = jnp.sin(ang) * sg_ref[...]


def _rope_call(positions):
    bsz, s = positions.shape
    m = bsz * s
    half = RET_QK // 2
    inv_freq = jnp.exp(-math.log(ROPE_BASE) * jnp.arange(half, dtype=F32) / half)
    f128 = jnp.tile(inv_freq, LANES // half).reshape(1, LANES)
    sign = np.tile(np.concatenate([-np.ones(half), np.ones(half)]), LANES // RET_QK)
    sign = jnp.asarray(sign.reshape(1, LANES), F32)
    pos = jnp.broadcast_to(positions.astype(F32).reshape(m, 1), (m, LANES))
    tm = min(1024, m)
    return pl.pallas_call(
        _rope_kernel,
        grid=(m // tm,),
        in_specs=[
            pl.BlockSpec((tm, LANES), lambda i: (i, 0)),
            pl.BlockSpec((1, LANES), lambda i: (0, 0)),
            pl.BlockSpec((1, LANES), lambda i: (0, 0)),
        ],
        out_specs=[pl.BlockSpec((tm, LANES), lambda i: (i, 0))] * 2,
        out_shape=[jax.ShapeDtypeStruct((m, LANES), F32)] * 2,
        compiler_params=_cp(("parallel",)),
        name="rope_tables",
    )(pos, f128, sign)


def _conv_silu(x, buf_ref, cw_ref, cb_ref):
    n = x.shape[0]
    buf_ref[CONV_PAD:CONV_PAD + n, :] = x
    acc = cb_ref[...] + cw_ref[CONV_K - 1:CONV_K, :] * x
    for k in range(CONV_K - 1):
        off = CONV_PAD - (CONV_K - 1) + k
        acc = acc + cw_ref[k:k + 1, :] * buf_ref[off:off + n, :]
    buf_ref[0:CONV_PAD, :] = x[n - CONV_PAD:n, :]
    return _silu(acc)


def _inproj_kernel(h_ref, nw_ref, sc_ref, sh_ref, w_ref, scw_ref, scb_ref, mcw_ref, mcb_ref,
                   o_ssd, o_ml, o_s5g, o_s5u, o_ret, o_sm, sbuf_ref, mbuf_ref, *, per_batch):
    @pl.when(pl.program_id(0) % per_batch == 0)
    def _():
        sbuf_ref[0:CONV_PAD, :] = jnp.zeros((CONV_PAD, sbuf_ref.shape[1]), F32)
        mbuf_ref[0:CONV_PAD, :] = jnp.zeros((CONV_PAD, mbuf_ref.shape[1]), F32)

    x = h_ref[...]
    hn = _rms(x) * nw_ref[...]
    hb = (hn * (1.0 + sc_ref[...]) + sh_ref[...]).astype(BF16)
    d = D_BRANCH

    def proj(col, width):
        return _dot(hb, w_ref[:, col:col + width])

    col = 0
    o_ssd[:, 0:d] = _silu(proj(col, d)).astype(BF16)
    o_ssd[:, d:3 * d] = _conv_silu(proj(col + d, 2 * d), sbuf_ref, scw_ref, scb_ref).astype(BF16)
    col += _W_SSD
    o_ml[:, 0:d] = _silu(proj(col, d)).astype(BF16)
    qk = _conv_silu(proj(col + d, 2 * d), mbuf_ref, mcw_ref, mcb_ref)
    o_ml[:, d:2 * d] = qk[:, 0:d].astype(BF16)
    o_ml[:, 2 * d:3 * d] = (qk[:, d:2 * d] * (ML_HEAD_DIM ** -0.5)).astype(BF16)
    o_ml[:, 3 * d:4 * d] = proj(col + 3 * d, d).astype(BF16)
    o_ml[:, 4 * d:5 * d] = _sigmoid(proj(col + 4 * d, d)).astype(BF16)
    col += _W_ML
    o_s5g[...] = _silu(proj(col, d)).astype(BF16)
    o_s5u[...] = proj(col + d, d).astype(BF16)
    col += _W_S5
    o_ret[:, 0:d] = _silu(proj(col, d)).astype(BF16)
    o_ret[:, d:3 * d] = proj(col + d, 2 * d).astype(BF16)
    col += _W_RET
    o_sm[...] = proj(col, _W_SM)


def _inproj_call(l, h, norm_w, mod5, w_all, ssd_cw, ssd_cb, ml_cw, ml_cb, s):
    m = h.shape[0]
    tm = min(ROW_TILE, s)
    per_batch = s // tm
    widths = (_W_SSD, _W_ML, _W_S5G, D_BRANCH, _W_RET, _W_SM)
    dtypes = (BF16, BF16, BF16, BF16, BF16, F32)
    cdim = 2 * D_BRANCH
    mod_spec = lambda which: pl.BlockSpec((None, None, None, 1, D_MODEL),
                                          lambda i: (l, i // per_batch, which, 0, 0))
    return pl.pallas_call(
        functools.partial(_inproj_kernel, per_batch=per_batch),
        grid=(m // tm,),
        in_specs=[
            pl.BlockSpec((tm, D_MODEL), lambda i: (i, 0)),
            _layer_spec(l, (1, D_MODEL), 1),
            mod_spec(1), mod_spec(0),
            pl.BlockSpec((None, D_MODEL, _W_ALL), lambda i: (l, 0, 0), pipeline_mode=pl.Buffered(1)),
            _layer_spec(l, (CONV_K, cdim), 1), _layer_spec(l, (1, cdim), 1),
            _layer_spec(l, (CONV_K, cdim), 1), _layer_spec(l, (1, cdim), 1),
        ],
        out_specs=[pl.BlockSpec((tm, n), lambda i: (i, 0)) for n in widths],
        out_shape=[jax.ShapeDtypeStruct((m, n), dt) for n, dt in zip(widths, dtypes)],
        scratch_shapes=[pltpu.VMEM((tm + CONV_PAD, cdim), F32), pltpu.VMEM((tm + CONV_PAD, cdim), F32)],
        compiler_params=_cp(("arbitrary",)),
        name="in_proj",
    )(h, norm_w, mod5, mod5, w_all, ssd_cw, ssd_cb, ml_cw, ml_cb)


def _ssd_kernel(p_ref, sm_ref, dtb_ref, alog_ref, dsk_ref, nw_ref, e8_ref, o_ref, st_ref):
    @pl.when(pl.program_id(0) == 0)
    def _():
        st_ref[...] = jnp.zeros(st_ref.shape, F32)

    tri = _tri_mask()
    tri_bf = jnp.where(tri, 1.0, 0.0).astype(BF16)
    for b in range(p_ref.shape[0]):
        _ssd_chunk(b, tri, tri_bf, p_ref, sm_ref, dtb_ref, alog_ref, dsk_ref, nw_ref, e8_ref, o_ref, st_ref)


def _ssd_chunk(b, tri, tri_bf, p_ref, sm_ref, dtb_ref, alog_ref, dsk_ref, nw_ref, e8_ref, o_ref, st_ref):
    e8 = e8_ref[...]
    gate = p_ref[b, :, 0:D_BRANCH].astype(F32)
    xs = p_ref[b, :, D_BRANCH:2 * D_BRANCH].astype(F32)
    gw = SSD_GROUPS * SSD_STATE
    bm = p_ref[b, :, 2 * D_BRANCH:2 * D_BRANCH + gw]
    cm = p_ref[b, :, 2 * D_BRANCH + gw:2 * D_BRANCH + 2 * gw]

    dt = _softplus(sm_ref[b] + dtb_ref[...])
    da = dt * (-jnp.exp(alog_ref[...]))
    a_cum = _dot01_l(tri_bf, da)
    a_cum_t = a_cum.T
    ac_e = _dot01_r(a_cum, e8)
    dt_e = _dot01_r(dt, e8)
    a_last_e = ac_e[CHUNK - 1:CHUNK, :]
    xd = xs * dt_e
    xd_bf = xd.astype(BF16)
    xw_bf = (xd * jnp.exp(a_last_e - ac_e)).astype(BF16)
    ea_e = jnp.exp(ac_e)

    hw = (SSD_HEADS // SSD_GROUPS) * SSD_HEAD_DIM
    lane = lax.broadcasted_iota(jnp.int32, (CHUNK, hw), 1)
    ys = []
    for g in range(SSD_GROUPS):
        bg = bm[:, g * SSD_STATE:(g + 1) * SSD_STATE]
        cg = cm[:, g * SSD_STATE:(g + 1) * SSD_STATE]
        cb = _dot_nt(cg, bg)
        s_prev = st_ref[b, g]
        y = _dot(cg, s_prev.astype(BF16)) * ea_e[:, g * hw:(g + 1) * hw]
        xd_g = xd_bf[:, g * hw:(g + 1) * hw]
        for hh in range(SSD_HEADS // SSD_GROUPS):
            h = g * (SSD_HEADS // SSD_GROUPS) + hh
            seg = a_cum[:, h:h + 1] - a_cum_t[h:h + 1, :]
            dec = jnp.exp(jnp.where(tri, seg, NEG_BIG))
            m_h = (cb * dec).astype(BF16)
            in_head = (lane >= hh * SSD_HEAD_DIM) & (lane < (hh + 1) * SSD_HEAD_DIM)
            y = y + _dot(m_h, jnp.where(in_head, xd_g, jnp.zeros_like(xd_g)))
        ys.append(y)
        s_loc = _dot_tn(bg, xw_bf[:, g * hw:(g + 1) * hw])
        st_ref[b, g] = s_prev * jnp.exp(a_last_e[:, g * hw:(g + 1) * hw]) + s_loc

    y = jnp.concatenate(ys, axis=1) + dsk_ref[...] * xs
    o_ref[b] = (_rms(y * gate) * nw_ref[...]).astype(o_ref.dtype)


def _ssd_call(l, p_ssd, p_sm, dt_bias, a_log, d_skip, norm_w, bsz, s):
    nc = s // CHUNK
    m = bsz * s
    e8 = np.zeros((LANES, D_BRANCH), np.float32)
    for h in range(SSD_HEADS):
        e8[h, h * SSD_HEAD_DIM:(h + 1) * SSD_HEAD_DIM] = 1.0
    row = lambda n: pl.BlockSpec((bsz, CHUNK, n), lambda c: (0, c, 0))
    return pl.pallas_call(
        _ssd_kernel,
        grid=(nc,),
        in_specs=[row(_W_SSD), row(LANES),
                  _layer_spec(l, (1, LANES), 2), _layer_spec(l, (1, LANES), 2),
                  _layer_spec(l, (1, D_BRANCH), 2), _layer_spec(l, (1, D_BRANCH), 2),
                  pl.BlockSpec((LANES, D_BRANCH), lambda c: (0, 0))],
        out_specs=row(D_BRANCH),
        out_shape=jax.ShapeDtypeStruct((bsz, s, D_BRANCH), BF16),
        scratch_shapes=[pltpu.VMEM((bsz, SSD_GROUPS, SSD_STATE, 256), F32)],
        compiler_params=_cp(("arbitrary",)),
        name="ssd_mixer",
    )(p_ssd.reshape(bsz, s, _W_SSD), p_sm.reshape(bsz, s, _W_SM), dt_bias, a_log, d_skip, norm_w,
      jnp.asarray(e8, BF16)).reshape(m, D_BRANCH)


def _mlstm_kernel(p_ref, sm_ref, ib_ref, fb_ref, nw_ref, o_ref, c_ref, m_ref):
    @pl.when(pl.program_id(0) == 0)
    def _():
        c_ref[...] = jnp.zeros(c_ref.shape, F32)
        m_ref[...] = jnp.zeros(m_ref.shape, F32)

    tri = _tri_mask()
    tri_bf = jnp.where(tri, 1.0, 0.0).astype(BF16)
    for b in range(p_ref.shape[0]):
        _mlstm_chunk(b, tri, tri_bf, p_ref, sm_ref, ib_ref, fb_ref, nw_ref, o_ref, c_ref, m_ref)


def _mlstm_chunk(b, tri, tri_bf, p_ref, sm_ref, ib_ref, fb_ref, nw_ref, o_ref, c_ref, m_ref):
    d = D_BRANCH
    gate = p_ref[b, :, 0:d].astype(F32)
    q = p_ref[b, :, d:2 * d]
    k = p_ref[b, :, 2 * d:3 * d]
    v = p_ref[b, :, 3 * d:4 * d]
    o_gate = p_ref[b, :, 4 * d:5 * d].astype(F32)

    i_log = sm_ref[b, :, LANES:2 * LANES] + ib_ref[...]
    f_log = -_softplus(-(sm_ref[b, :, 2 * LANES:3 * LANES] + fb_ref[...]))
    f_cum = _dot01_l(tri_bf, f_log)
    r = i_log - f_cum
    r_t = r.T
    f_last = f_cum[CHUNK - 1:CHUNK, :]
    m_prev = m_ref[b]
    a_loc = f_last + r
    m_new = jnp.maximum(f_last + m_prev, jnp.max(a_loc, axis=0, keepdims=True))
    w_loc = jnp.exp(a_loc - m_new)
    s_old = jnp.exp(f_last + m_prev - m_new)
    m_ref[b] = m_new

    ones = jnp.ones((CHUNK, ML_HEAD_DIM), BF16)
    outs = []
    for h in range(ML_HEADS):
        sl = slice(h * ML_HEAD_DIM, (h + 1) * ML_HEAD_DIM)
        qh = q[:, sl]
        kh = k[:, sl]
        v_ext = jnp.concatenate([v[:, sl], ones], axis=1)
        g_col = f_cum[:, h:h + 1]
        dmat = jnp.where(tri, g_col + r_t[h:h + 1, :], NEG_BIG)
        g_inter = g_col + m_prev[:, h:h + 1]
        m_t = jnp.maximum(g_inter, jnp.max(dmat, axis=-1, keepdims=True))
        sm = (_dot_nt(qh, kh) * jnp.exp(dmat - m_t)).astype(BF16)
        c_prev = c_ref[b, h]
        numden = _dot(sm, v_ext) + jnp.exp(g_inter - m_t) * _dot(qh, c_prev.astype(BF16))
        num = numden[:, 0:ML_HEAD_DIM]
        den = numden[:, ML_HEAD_DIM:]
        hh = num / jnp.maximum(jnp.abs(den), jnp.exp(-m_t))
        outs.append(_rms(o_gate[:, sl] * hh))
        kw = (kh.astype(F32) * w_loc[:, h:h + 1]).astype(BF16)
        c_ref[b, h] = s_old[:, h:h + 1] * c_prev + _dot_tn(kw, v_ext)

    y = jnp.concatenate(outs, axis=1) * nw_ref[...] * gate
    o_ref[b] = y.astype(o_ref.dtype)


def _mlstm_call(l, p_ml, p_sm, i_bias, f_bias, norm_w, bsz, s):
    nc = s // CHUNK
    m = bsz * s
    row = lambda n: pl.BlockSpec((bsz, CHUNK, n), lambda c: (0, c, 0))
    return pl.pallas_call(
        _mlstm_kernel,
        grid=(nc,),
        in_specs=[row(_W_ML), row(_W_SM),
                  _layer_spec(l, (1, LANES), 2), _layer_spec(l, (1, LANES), 2),
                  _layer_spec(l, (1, D_BRANCH), 2)],
        out_specs=row(D_BRANCH),
        out_shape=jax.ShapeDtypeStruct((bsz, s, D_BRANCH), BF16),
        scratch_shapes=[pltpu.VMEM((bsz, ML_HEADS, ML_HEAD_DIM, 2 * ML_HEAD_DIM), F32),
                        pltpu.VMEM((bsz, 1, LANES), F32)],
        compiler_params=_cp(("arbitrary",)),
        name="mlstm_mixer",
    )(p_ml.reshape(bsz, s, _W_ML), p_sm.reshape(bsz, s, _W_SM), i_bias, f_bias,
      norm_w).reshape(m, D_BRANCH)


def _ret_log_gamma():
    return [math.log1p(-2.0 ** (-(RET_DECAY_BASE + h))) for h in range(RET_HEADS)]


def _ret_consts():
    lg = _ret_log_gamma()
    pos = np.arange(CHUNK, dtype=np.float64)
    rel = pos[:, None] - pos[None, :]
    dm = np.stack([np.where(rel >= 0, np.exp(np.maximum(rel, 0.0) * g), 0.0) for g in lg])
    from_start = np.concatenate(
        [np.repeat(np.exp((pos + 1.0) * g)[:, None], RET_V, axis=1) for g in lg], axis=1)
    to_end = np.concatenate(
        [np.repeat(np.exp((CHUNK - 1.0 - pos) * g)[:, None], RET_QK, axis=1) for g in lg], axis=1)
    qw, vw = RET_HEADS * RET_QK, RET_HEADS * RET_V
    chunk_decay = np.zeros((qw, vw))
    for h, g in enumerate(lg):
        chunk_decay[h * RET_QK:(h + 1) * RET_QK, h * RET_V:(h + 1) * RET_V] = math.exp(CHUNK * g)
    block_diag = (chunk_decay > 0).astype(np.float64)
    swap = np.zeros((qw, qw))
    for j in range(qw):
        base, off = (j // RET_QK) * RET_QK, j % RET_QK
        swap[base + (off + RET_QK // 2) % RET_QK, j] = 1.0
    f = lambda a: jnp.asarray(a, F32)
    return f(dm), f(from_start), f(to_end), f(chunk_decay), f(block_diag), jnp.asarray(swap, BF16)


def _ret_kernel(p_ref, cos_ref, sin_ref, dm_ref, fs_ref, te_ref, cd_ref, bd_ref, sw_ref, nw_ref,
                o_ref, r_ref):
    @pl.when(pl.program_id(0) == 0)
    def _():
        r_ref[...] = jnp.zeros(r_ref.shape, F32)

    for b in range(p_ref.shape[0]):
        _ret_chunk(b, p_ref, cos_ref, sin_ref, dm_ref, fs_ref, te_ref, cd_ref, bd_ref, sw_ref, nw_ref,
                   o_ref, r_ref)


def _ret_chunk(b, p_ref, cos_ref, sin_ref, dm_ref, fs_ref, te_ref, cd_ref, bd_ref, sw_ref, nw_ref,
               o_ref, r_ref):
    d = D_BRANCH
    qw = RET_HEADS * RET_QK
    gate = p_ref[b, :, 0:d].astype(F32)
    q_raw = p_ref[b, :, d:d + qw]
    k_raw = p_ref[b, :, d + qw:d + 2 * qw]
    v = p_ref[b, :, d + 2 * qw:2 * d + 2 * qw]
    cos_e = jnp.concatenate([cos_ref[b]] * (qw // LANES), axis=1)
    sin_e = jnp.concatenate([sin_ref[b]] * (qw // LANES), axis=1)
    sw = sw_ref[...]
    q = q_raw.astype(F32) * cos_e + _dot(q_raw, sw) * sin_e
    k = (k_raw.astype(F32) * cos_e + _dot(k_raw, sw) * sin_e) * (RET_QK ** -0.5)
    k_bf = k.astype(BF16)

    r_prev = r_ref[b]
    y = _dot(q.astype(BF16), r_prev.astype(BF16)) * fs_ref[...]
    lane = lax.broadcasted_iota(jnp.int32, (CHUNK, qw), 1)
    inner = []
    for h in range(RET_HEADS):
        in_head = (lane >= h * RET_QK) & (lane < (h + 1) * RET_QK)
        qh = jnp.where(in_head, q, jnp.zeros_like(q)).astype(BF16)
        scores = (_dot_nt(qh, k_bf) * dm_ref[h]).astype(BF16)
        inner.append(_dot(scores, v[:, h * RET_V:(h + 1) * RET_V]))
    y = y + jnp.concatenate(inner, axis=1)
    r_ref[b] = r_prev * cd_ref[...] + _dot_tn((k * te_ref[...]).astype(BF16), v) * bd_ref[...]

    outs = [_rms(y[:, h * RET_V:(h + 1) * RET_V]) for h in range(RET_HEADS)]
    o_ref[b] = (jnp.concatenate(outs, axis=1) * nw_ref[...] * gate).astype(o_ref.dtype)


def _ret_call(l, p_ret, cos_t, sin_t, consts, norm_w, bsz, s):
    nc = s // CHUNK
    m = bsz * s
    qw, vw = RET_HEADS * RET_QK, RET_HEADS * RET_V
    dm, fs, te, cd, bd, sw = consts
    const = lambda shape: pl.BlockSpec(shape, lambda c: (0,) * len(shape))
    row = lambda n: pl.BlockSpec((bsz, CHUNK, n), lambda c: (0, c, 0))
    return pl.pallas_call(
        _ret_kernel,
        grid=(nc,),
        in_specs=[row(_W_RET), row(LANES), row(LANES),
                  const((RET_HEADS, CHUNK, CHUNK)), const((CHUNK, vw)), const((CHUNK, qw)),
                  const((qw, vw)), const((qw, vw)), const((qw, qw)),
                  _layer_spec(l, (1, D_BRANCH), 2)],
        out_specs=row(D_BRANCH),
        out_shape=jax.ShapeDtypeStruct((bsz, s, D_BRANCH), BF16),
        scratch_shapes=[pltpu.VMEM((bsz, qw, vw), F32)],
        compiler_params=_cp(("arbitrary",)),
        name="retention_mixer",
    )(p_ret.reshape(bsz, s, _W_RET), cos_t.reshape(bsz, s, LANES), sin_t.reshape(bsz, s, LANES),
      dm, fs, te, cd, bd, sw, norm_w).reshape(m, D_BRANCH)


def _s5_compact(lam_re, lam_im, b_re, b_im, c_re, c_im, log_step):
    depth = lam_re.shape[0]
    nb, ngl, sub = S5_LANE_BLOCKS, S5_LOCAL_GROUPS, S5_SUB
    step = jnp.exp(log_step.astype(F32))[..., None]
    lr = jnp.minimum(lam_re.astype(F32), -1e-4)
    li = lam_im.astype(F32)
    mag = jnp.exp(lr * step)
    ang = li * step
    ab_re = mag * jnp.cos(ang)
    ab_im = mag * jnp.sin(ang)
    den = lr * lr + li * li
    coef_re = ((ab_re - 1.0) * lr + ab_im * li) / den
    coef_im = (ab_im * lr - (ab_re - 1.0) * li) / den
    br, bi = b_re.astype(F32), b_im.astype(F32)
    bb_re = coef_re[..., None] * br - coef_im[..., None] * bi
    bb_im = coef_re[..., None] * bi + coef_im[..., None] * br
    pw_re, pw_im = [jnp.ones_like(ab_re)], [jnp.zeros_like(ab_im)]
    for _ in range(sub):
        pr, pi = pw_re[-1], pw_im[-1]
        pw_re.append(pr * ab_re - pi * ab_im)
        pw_im.append(pr * ab_im + pi * ab_re)
    pw_re, pw_im = jnp.stack(pw_re), jnp.stack(pw_im)
    cr, ci = c_re.astype(F32), c_im.astype(F32)
    hp = lax.Precision.HIGHEST
    ab_b_re = pw_re[..., None] * bb_re[None] - pw_im[..., None] * bb_im[None]
    ab_b_im = pw_re[..., None] * bb_im[None] + pw_im[..., None] * bb_re[None]
    ca_re = cr[None] * pw_re[:, :, :, None, :] - ci[None] * pw_im[:, :, :, None, :]
    ca_im = cr[None] * pw_im[:, :, :, None, :] + ci[None] * pw_re[:, :, :, None, :]
    taps = (jnp.einsum('lgop,klgpi->klgoi', cr, ab_b_re[:sub], precision=hp)
            - jnp.einsum('lgop,klgpi->klgoi', ci, ab_b_im[:sub], precision=hp))

    def rows_in(a):
        a = a[:sub][::-1].reshape(sub, depth, nb, ngl, S5_STATE, S5_GROUP)
        return a.transpose(1, 2, 0, 3, 5, 4).reshape(depth, nb, sub * ngl * S5_GROUP, S5_STATE)

    c_in = jnp.concatenate([rows_in(ab_b_re), rows_in(ab_b_im)], axis=-1)
    lag = np.arange(sub)[None, :] - np.arange(sub)[:, None]
    toe = taps[np.maximum(lag, 0)] * jnp.asarray(lag >= 0, F32)[:, :, None, None, None, None]
    toe = toe.reshape(sub, sub, depth, nb, ngl, S5_GROUP, S5_GROUP)
    toe = toe.transpose(2, 3, 0, 4, 6, 1, 5).reshape(depth, nb, sub * ngl * S5_GROUP, sub * S5_GROUP)

    def rows_x(a):
        a = a[1:].reshape(sub, depth, nb, ngl, S5_GROUP, S5_STATE)
        return a.transpose(1, 2, 3, 5, 0, 4).reshape(depth, nb, ngl * S5_STATE, sub * S5_GROUP)

    c_out = jnp.concatenate([toe, rows_x(ca_re), -rows_x(ca_im)], axis=2)
    a8 = jnp.concatenate([pw_re[sub].reshape(depth, nb, 1, ngl * S5_STATE),
                          pw_im[sub].reshape(depth, nb, 1, ngl * S5_STATE)], axis=-1)
    return c_in.astype(BF16), c_out.astype(BF16), a8


def _s5_expanders():
    kin = S5_SUB * LANES
    half = S5_LOCAL_GROUPS * S5_STATE
    sel_in = np.zeros((LANES, 2 * half), np.float32)
    csel_in = np.zeros((LANES, 2 * half), np.float32)
    for col in range(2 * half):
        sel_in[(col // half) * S5_STATE + col % S5_STATE, col] = 1.0
        csel_in[(col % half) // S5_STATE, col] = 1.0
    sel_out = np.zeros((LANES, kin), np.float32)
    csel_out = np.zeros((LANES, kin), np.float32)
    for col in range(kin):
        sel_out[(col // LANES) * S5_GROUP + col % S5_GROUP, col] = 1.0
        csel_out[(col // S5_GROUP) % S5_LOCAL_GROUPS, col] = 1.0
    rsel_in = np.zeros((kin, LANES), np.float32)
    for r in range(kin):
        rsel_in[r, (r // S5_GROUP) % S5_LOCAL_GROUPS] = 1.0
    rsel_out = np.zeros((2 * kin, LANES), np.float32)
    rsel_out[:kin] = rsel_in
    for r in range(kin):
        rsel_out[kin + r, (r % half) // S5_STATE] = 1.0
    return tuple(jnp.asarray(a, BF16) for a in (sel_in, csel_in, rsel_in, sel_out, csel_out, rsel_out))


def _s5_fold(a, bsz, s):
    j = s // S5_SUB
    a = a.reshape(bsz, j, S5_SUB, S5_LANE_BLOCKS, LANES).transpose(1, 0, 3, 2, 4)
    return a.reshape(j * bsz, S5_LANE_BLOCKS * S5_SUB * LANES)


def _s5_unfold(a, bsz, s):
    j = s // S5_SUB
    a = a.reshape(j, bsz, S5_LANE_BLOCKS, S5_SUB, LANES).transpose(1, 0, 3, 2, 4)
    return a.reshape(bsz * s, D_BRANCH)


S5_ROW_BLOCK = 512


def _s5_scan_kernel(u_ref, cin_ref, cout_ref, a8_ref, sin_ref, csin_ref, rsin_ref,
                    sout_ref, csout_ref, rsout_ref, o_ref, x_ref, min_ref, mout_ref, *, bsz):
    rows = u_ref.shape[0]
    half = S5_LOCAL_GROUPS * S5_STATE
    rb = min(S5_ROW_BLOCK, rows)

    min_ref[...] = (_dot(cin_ref[...], sin_ref[...]) * _dot(rsin_ref[...], csin_ref[...])).astype(BF16)
    mout_ref[...] = (_dot(cout_ref[...], sout_ref[...]) * _dot(rsout_ref[...], csout_ref[...])).astype(BF16)

    for r0 in range(0, rows, rb):
        x_ref[r0:r0 + rb, :] = _dot(u_ref[r0:r0 + rb, :], min_ref[...])

    a_re = a8_ref[:, 0:half]
    a_im = a8_ref[:, half:]

    def body(i, carry):
        x_re, x_im = carry
        r0 = pl.multiple_of(i * bsz, bsz)
        loc = x_ref[pl.ds(r0, bsz), :]
        x_ref[pl.ds(r0, bsz), :] = jnp.concatenate([x_re, x_im], axis=1)
        n_re = a_re * x_re - a_im * x_im + loc[:, 0:half]
        n_im = a_re * x_im + a_im * x_re + loc[:, half:]
        return n_re, n_im

    zero = jnp.zeros((bsz, half), F32)
    lax.fori_loop(0, rows // bsz, body, (zero, zero), unroll=8)

    for r0 in range(0, rows, rb):
        lhs = jnp.concatenate([u_ref[r0:r0 + rb, :], x_ref[r0:r0 + rb, :].astype(BF16)], axis=1)
        o_ref[r0:r0 + rb, :] = _dot(lhs, mout_ref[...]).astype(o_ref.dtype)


def _s5_scan_call(l, u_fold, c_in, c_out, a8, expanders, bsz):
    rows = u_fold.shape[0]
    kin = S5_SUB * LANES
    lspec = lambda shape: pl.BlockSpec((None, None) + shape, lambda lb: (l, lb, 0, 0))
    const = lambda a: pl.BlockSpec(a.shape, lambda lb: (0, 0))
    sel_in, csel_in, rsel_in, sel_out, csel_out, rsel_out = expanders
    return pl.pallas_call(
        functools.partial(_s5_scan_kernel, bsz=bsz),
        grid=(S5_LANE_BLOCKS,),
        in_specs=[pl.BlockSpec((rows, kin), lambda lb: (0, lb)),
                  lspec((kin, LANES)), lspec((2 * kin, LANES)), lspec((1, kin)),
                  const(sel_in), const(csel_in), const(rsel_in),
                  const(sel_out), const(csel_out), const(rsel_out)],
        out_specs=pl.BlockSpec((rows, kin), lambda lb: (0, lb)),
        out_shape=jax.ShapeDtypeStruct(u_fold.shape, BF16),
        scratch_shapes=[pltpu.VMEM((rows, kin), F32),
                        pltpu.VMEM((kin, kin), BF16), pltpu.VMEM((2 * kin, kin), BF16)],
        compiler_params=_cp(("parallel",)),
        name="s5_scan",
    )(u_fold, c_in, c_out, a8, sel_in, csel_in, rsel_in, sel_out, csel_out, rsel_out)


def _s5_post_kernel(g_ref, u_ref, y_ref, d_ref, wg_ref, bg_ref, nw_ref, o_ref):
    d = D_BRANCH
    gate = g_ref[...].astype(F32)
    u = u_ref[...].astype(F32)
    y = y_ref[...].astype(F32) + d_ref[...] * u
    y = 0.5 * y * (1.0 + jnp.tanh(math.sqrt(2.0 / math.pi) * (y + 0.044715 * (y * y * y))))
    g = _dot(y.astype(BF16), wg_ref[...]) + bg_ref[...]
    y = g[:, 0:d] * _sigmoid(g[:, d:2 * d])
    o_ref[...] = (_rms(y) * nw_ref[...] * gate).astype(o_ref.dtype)


def _s5_post_call(l, p_gate, p_u, y_ssm, d_skip, w_glu, b_glu, norm_w):
    m = p_u.shape[0]
    tm = min(ROW_TILE, m)
    return pl.pallas_call(
        _s5_post_kernel,
        grid=(m // tm,),
        in_specs=[pl.BlockSpec((tm, D_BRANCH), lambda i: (i, 0)),
                  pl.BlockSpec((tm, D_BRANCH), lambda i: (i, 0)),
                  pl.BlockSpec((tm, D_BRANCH), lambda i: (i, 0)),
                  _layer_spec(l, (1, D_BRANCH), 1), _layer_spec(l, (D_BRANCH, 2 * D_BRANCH), 1),
                  _layer_spec(l, (1, 2 * D_BRANCH), 1), _layer_spec(l, (1, D_BRANCH), 1)],
        out_specs=pl.BlockSpec((tm, D_BRANCH), lambda i: (i, 0)),
        out_shape=jax.ShapeDtypeStruct((m, D_BRANCH), BF16),
        compiler_params=_cp(("parallel",)),
        name="s5_post",
    )(p_gate, p_u, y_ssm, d_skip, w_glu, b_glu, norm_w)


def _outproj_kernel(y1, y2, y3, y4, w_ref, h_ref, g_ref, fw_ref, o_ref, *, final_norm):
    acc = _dot(y1[...], w_ref[0 * D_BRANCH:1 * D_BRANCH, :])
    acc = acc + _dot(y2[...], w_ref[1 * D_BRANCH:2 * D_BRANCH, :])
    acc = acc + _dot(y3[...], w_ref[2 * D_BRANCH:3 * D_BRANCH, :])
    acc = acc + _dot(y4[...], w_ref[3 * D_BRANCH:4 * D_BRANCH, :])
    h = h_ref[...] + g_ref[...] * acc
    if final_norm:
        h = _rms(h) * fw_ref[...]
    o_ref[...] = h


def _outproj_call(l, ys, w_out, h, mod5, final_w, s, final_norm):
    m = h.shape[0]
    tm = min(ROW_TILE, s)
    per_batch = s // tm
    yspec = pl.BlockSpec((tm, D_BRANCH), lambda i: (i, 0))
    return pl.pallas_call(
        functools.partial(_outproj_kernel, final_norm=final_norm),
        grid=(m // tm,),
        in_specs=[yspec] * 4 + [
            _layer_spec(l, (4 * D_BRANCH, D_MODEL), 1),
            pl.BlockSpec((tm, D_MODEL), lambda i: (i, 0)),
            pl.BlockSpec((None, None, None, 1, D_MODEL), lambda i: (l, i // per_batch, 2, 0, 0)),
            pl.BlockSpec((1, D_MODEL), lambda i: (0, 0)),
        ],
        out_specs=pl.BlockSpec((tm, D_MODEL), lambda i: (i, 0)),
        out_shape=jax.ShapeDtypeStruct((m, D_MODEL), F32),
        compiler_params=_cp(("parallel",)),
        name="out_proj",
    )(*ys, w_out, h, mod5, final_w)


def _pack_w_in(w_in):
    pad = lambda a: jnp.pad(a, [(0, 0)] * (a.ndim - 1) + [(0, LANES - a.shape[-1])])
    parts = [w_in[..., _O_SSD_Z:_O_SSD_DT], w_in[..., _O_ML_Z:_O_ML_I], w_in[..., _O_S5_Z:_O_RET_Z],
             w_in[..., _O_RET_Z:_D_IN], pad(w_in[..., _O_SSD_DT:_O_ML_Z]),
             pad(w_in[..., _O_ML_I:_O_ML_F]), pad(w_in[..., _O_ML_F:_O_S5_Z])]
    return jnp.concatenate(parts, axis=-1).astype(BF16)


def _pad_lanes(v):
    return jnp.pad(v.astype(F32), ((0, 0), (0, LANES - v.shape[-1])))[:, None, :]


def kernel(x, c, positions, norm_w, w_ada, b_ada, w_in, w_out, ssd_conv_w, ssd_conv_b, ssd_dt_bias, ssd_a_log, ssd_d, ssd_norm_w, ml_conv_w, ml_conv_b, ml_i_bias, ml_f_bias, ml_norm_w, s5_lambda_re, s5_lambda_im, s5_b_re, s5_b_im, s5_c_re, s5_c_im, s5_d, s5_log_step, s5_w_glu, s5_b_glu, s5_norm_w, ret_norm_w, final_norm_w):
    bsz, s, d = x.shape
    depth = w_in.shape[0]
    assert d == D_MODEL and s % CHUNK == 0 and bsz % 8 == 0
    m = bsz * s
    row3 = lambda a: a.astype(F32)[:, None, :]

    mod = _mod_call(c.astype(F32), w_ada, b_ada)
    mod5 = mod.reshape(depth, bsz, 3, 1, D_MODEL)
    cos_t, sin_t = _rope_call(positions)
    ret_consts = _ret_consts()
    w_all = _pack_w_in(w_in)
    w_out_bf = w_out.astype(BF16)
    w_glu_bf = s5_w_glu.astype(BF16)
    c_in, c_out, a8 = _s5_compact(s5_lambda_re, s5_lambda_im, s5_b_re, s5_b_im, s5_c_re, s5_c_im,
                                  s5_log_step)
    s5_exp = _s5_expanders()
    norm_w3, final_w = row3(norm_w), final_norm_w.reshape(1, D_MODEL).astype(F32)
    ssd_cb3, ml_cb3 = row3(ssd_conv_b), row3(ml_conv_b)
    dt_bias3, a_log3 = _pad_lanes(ssd_dt_bias), _pad_lanes(ssd_a_log)
    d_skip3 = row3(jnp.repeat(ssd_d, SSD_HEAD_DIM, axis=-1))
    i_bias3, f_bias3 = _pad_lanes(ml_i_bias), _pad_lanes(ml_f_bias)
    ssd_nw3, ml_nw3, s5_nw3, ret_nw3 = row3(ssd_norm_w), row3(ml_norm_w), row3(s5_norm_w), row3(ret_norm_w)
    s5_d3, b_glu3 = row3(s5_d), row3(s5_b_glu)

    h = x.reshape(m, D_MODEL)
    for l in range(depth):
        p_ssd, p_ml, p_s5g, p_s5u, p_ret, p_sm = _inproj_call(
            l, h, norm_w3, mod5, w_all, ssd_conv_w, ssd_cb3, ml_conv_w, ml_cb3, s)
        y_ssd = _ssd_call(l, p_ssd, p_sm, dt_bias3, a_log3, d_skip3, ssd_nw3, bsz, s)
        y_ml = _mlstm_call(l, p_ml, p_sm, i_bias3, f_bias3, ml_nw3, bsz, s)
        y_fold = _s5_scan_call(l, _s5_fold(p_s5u, bsz, s), c_in, c_out, a8, s5_exp, bsz)
        y_s5 = _s5_post_call(l, p_s5g, p_s5u, _s5_unfold(y_fold, bsz, s), s5_d3, w_glu_bf, b_glu3, s5_nw3)
        y_ret = _ret_call(l, p_ret, cos_t, sin_t, ret_consts, ret_nw3, bsz, s)
        h = _outproj_call(l, (y_ssd, y_ml, y_s5, y_ret), w_out_bf, h, mod5, final_w, s,
                          final_norm=(l == depth - 1))
    return h.reshape(bsz, s, D_MODEL)
```

```python
import functools
import math

import numpy as np
import jax
import jax.numpy as jnp
from jax import lax
from jax.experimental import pallas as pl
from jax.experimental.pallas import tpu as pltpu

F32 = jnp.float32
BF16 = jnp.bfloat16

D_MODEL = 1024
D_BRANCH = 512
CHUNK = 128
CONV_K = 4
CONV_PAD = 8
NORM_EPS = 1e-6
NEG_BIG = -1e30

SSD_HEADS = 8
SSD_HEAD_DIM = 64
SSD_GROUPS = 2
SSD_STATE = 128
ML_HEADS = 4
ML_HEAD_DIM = 128
S5_GROUPS = 32
S5_GROUP = 16
S5_STATE = 64
S5_SUB = 8
S5_LANE_BLOCKS = 4
S5_LOCAL_GROUPS = S5_GROUPS // S5_LANE_BLOCKS
RET_HEADS = 4
RET_QK = 64
RET_V = 128
RET_DECAY_BASE = 5.0
ROPE_BASE = 10000.0
LANES = 128

_O_SSD_Z, _O_SSD_DT = 0, 1536
_O_ML_Z, _O_ML_I, _O_ML_F = 1544, 4104, 4108
_O_S5_Z = 4112
_O_RET_Z = 5136
_D_IN = 6672
_W_SSD, _W_ML, _W_S5, _W_RET, _W_SM = 1536, 2560, 1024, 1536, 384
_W_ALL = _W_SSD + _W_ML + _W_S5 + _W_RET + _W_SM
_W_S5G = _W_S5 - D_BRANCH

VMEM_LIMIT = 56 * 1024 * 1024
ROW_TILE = 512


def _cp(sem):
    return pltpu.CompilerParams(dimension_semantics=sem, vmem_limit_bytes=VMEM_LIMIT)


def _layer_spec(l, shape, nargs):
    return pl.BlockSpec((None,) + tuple(shape), lambda *_: (l,) + (0,) * len(shape))


def _dot(a, b):
    return jnp.dot(a, b, preferred_element_type=F32)


def _dot_nt(a, b):
    return lax.dot_general(a, b, (((1,), (1,)), ((), ())), preferred_element_type=F32)


def _dot_tn(a, b):
    return lax.dot_general(a, b, (((0,), (0,)), ((), ())), preferred_element_type=F32)


def _sigmoid(x):
    return 0.5 + 0.5 * jnp.tanh(0.5 * x)


def _silu(x):
    hx = 0.5 * x
    return hx + hx * jnp.tanh(hx)


def _softplus(x):
    return jnp.maximum(x, 0.0) + jnp.log1p(jnp.exp(-jnp.abs(x)))


def _split3(v):
    hi = v.astype(BF16)
    r1 = v - hi.astype(F32)
    mid = r1.astype(BF16)
    lo = (r1 - mid.astype(F32)).astype(BF16)
    return hi, mid, lo


def _dot01_r2(v, m01):
    hi = v.astype(BF16)
    lo = (v - hi.astype(F32)).astype(BF16)
    return _dot(hi, m01) + _dot(lo, m01)


def _cumsum_rows_batched(tri_bf, xs):
    terms = []
    for x in xs:
        terms.extend(_split3(x))
    res = _dot(tri_bf, jnp.concatenate(terms, axis=1))
    w = xs[0].shape[1]
    return [res[:, (3 * i) * w:(3 * i + 1) * w] + res[:, (3 * i + 1) * w:(3 * i + 2) * w]
            + res[:, (3 * i + 2) * w:(3 * i + 3) * w] for i in range(len(xs))]


def _cummax_rows(x):
    n = x.shape[0]
    row = lax.broadcasted_iota(jnp.int32, x.shape, 0)
    sh = 1
    while sh < n:
        x = jnp.maximum(x, jnp.where(row >= sh, pltpu.roll(x, sh, axis=0), NEG_BIG))
        sh *= 2
    return x


def _tri_mask():
    row = lax.broadcasted_iota(jnp.int32, (CHUNK, CHUNK), 0)
    col = lax.broadcasted_iota(jnp.int32, (CHUNK, CHUNK), 1)
    return row >= col


def _rms(y):
    return y * lax.rsqrt(jnp.mean(y * y, axis=-1, keepdims=True) + NORM_EPS)


def _mod_kernel(c_ref, w_ref, b_ref, o_ref):
    cond = _silu(c_ref[...])
    o_ref[0] = _dot(cond.astype(BF16), w_ref[0].astype(BF16)) + b_ref[0]


def _mod_call(c, w_ada, b_ada):
    depth, d, n = w_ada.shape
    bsz = c.shape[0]
    tn = 1024
    return pl.pallas_call(
        _mod_kernel,
        grid=(depth, n // tn),
        in_specs=[
            pl.BlockSpec((bsz, d), lambda l, j: (0, 0)),
            pl.BlockSpec((1, d, tn), lambda l, j: (l, 0, j)),
            pl.BlockSpec((1, 1, tn), lambda l, j: (l, 0, j)),
        ],
        out_specs=pl.BlockSpec((1, bsz, tn), lambda l, j: (l, 0, j)),
        out_shape=jax.ShapeDtypeStruct((depth, bsz, n), F32),
        compiler_params=_cp(("parallel", "parallel")),
        name="adaln_mod",
    )(c, w_ada, b_ada.reshape(depth, 1, n))


def _rope_kernel(pos_ref, f_ref, sg_ref, cos_ref, sin_ref):
    ang = pos_ref[...] * f_ref[...]
    cos_ref[...] = jnp.cos(ang)
    sin_ref[...] = jnp.sin(ang) * sg_ref[...]


def _rope_call(positions):
    bsz, s = positions.shape
    m = bsz * s
    half = RET_QK // 2
    inv_freq = jnp.exp(-math.log(ROPE_BASE) * jnp.arange(half, dtype=F32) / half)
    f128 = jnp.tile(inv_freq, LANES // half).reshape(1, LANES)
    sign = np.tile(np.concatenate([-np.ones(half), np.ones(half)]), LANES // RET_QK)
    sign = jnp.asarray(sign.reshape(1, LANES), F32)
    pos = jnp.broadcast_to(positions.astype(F32).reshape(m, 1), (m, LANES))
    tm = min(1024, m)
    return pl.pallas_call(
        _rope_kernel,
        grid=(m // tm,),
        in_specs=[
            pl.BlockSpec((tm, LANES), lambda i: (i, 0)),
            pl.BlockSpec((1, LANES), lambda i: (0, 0)),
            pl.BlockSpec((1, LANES), lambda i: (0, 0)),
        ],
        out_specs=[pl.BlockSpec((tm, LANES), lambda i: (i, 0))] * 2,
        out_shape=[jax.ShapeDtypeStruct((m, LANES), F32)] * 2,
        compiler_params=_cp(("parallel",)),
        name="rope_tables",
    )(pos, f128, sign)


def _conv_silu(x, buf_ref, cw_ref, cb_ref):
    n = x.shape[0]
    buf_ref[CONV_PAD:CONV_PAD + n, :] = x
    acc = cb_ref[...] + cw_ref[CONV_K - 1:CONV_K, :] * x
    for k in range(CONV_K - 1):
        off = CONV_PAD - (CONV_K - 1) + k
        acc = acc + cw_ref[k:k + 1, :] * buf_ref[off:off + n, :]
    buf_ref[0:CONV_PAD, :] = x[n - CONV_PAD:n, :]
    return _silu(acc)


def _inproj_kernel(h_ref, nw_ref, sc_ref, sh_ref, w_ref, scw_ref, scb_ref, mcw_ref, mcb_ref,
                   o_ssd, o_ml, o_s5g, o_s5u, o_ret, o_sm, sbuf_ref, mbuf_ref, *, per_batch):
    @pl.when(pl.program_id(0) % per_batch == 0)
    def _():
        sbuf_ref[0:CONV_PAD, :] = jnp.zeros((CONV_PAD, sbuf_ref.shape[1]), F32)
        mbuf_ref[0:CONV_PAD, :] = jnp.zeros((CONV_PAD, mbuf_ref.shape[1]), F32)

    x = h_ref[...]
    hn = _rms(x) * nw_ref[...]
    hb = (hn * (1.0 + sc_ref[...]) + sh_ref[...]).astype(BF16)
    d = D_BRANCH

    def proj(col, width):
        return _dot(hb, w_ref[:, col:col + width])

    col = 0
    o_ssd[:, 0:d] = _silu(proj(col, d)).astype(BF16)
    o_ssd[:, d:3 * d] = _conv_silu(proj(col + d, 2 * d), sbuf_ref, scw_ref, scb_ref).astype(BF16)
    col += _W_SSD
    o_ml[:, 0:d] = _silu(proj(col, d)).astype(BF16)
    qk = _conv_silu(proj(col + d, 2 * d), mbuf_ref, mcw_ref, mcb_ref)
    o_ml[:, d:2 * d] = qk[:, 0:d].astype(BF16)
    o_ml[:, 2 * d:3 * d] = (qk[:, d:2 * d] * (ML_HEAD_DIM ** -0.5)).astype(BF16)
    o_ml[:, 3 * d:4 * d] = proj(col + 3 * d, d).astype(BF16)
    o_ml[:, 4 * d:5 * d] = _sigmoid(proj(col + 4 * d, d)).astype(BF16)
    col += _W_ML
    o_s5g[...] = _silu(proj(col, d)).astype(BF16)
    o_s5u[...] = proj(col + d, d).astype(BF16)
    col += _W_S5
    o_ret[:, 0:d] = _silu(proj(col, d)).astype(BF16)
    o_ret[:, d:3 * d] = proj(col + d, 2 * d).astype(BF16)
    col += _W_RET
    o_sm[...] = proj(col, _W_SM)


def _inproj_call(l, h, norm_w, mod5, w_all, ssd_cw, ssd_cb, ml_cw, ml_cb, s):
    m = h.shape[0]
    tm = min(ROW_TILE, s)
    per_batch = s // tm
    widths = (_W_SSD, _W_ML, _W_S5G, D_BRANCH, _W_RET, _W_SM)
    dtypes = (BF16, BF16, BF16, BF16, BF16, F32)
    cdim = 2 * D_BRANCH
    mod_spec = lambda which: pl.BlockSpec((None, None, None, 1, D_MODEL),
                                          lambda i: (l, i // per_batch, which, 0, 0))
    return pl.pallas_call(
        functools.partial(_inproj_kernel, per_batch=per_batch),
        grid=(m // tm,),
        in_specs=[
            pl.BlockSpec((tm, D_MODEL), lambda i: (i, 0)),
            _layer_spec(l, (1, D_MODEL), 1),
            mod_spec(1), mod_spec(0),
            pl.BlockSpec((None, D_MODEL, _W_ALL), lambda i: (l, 0, 0), pipeline_mode=pl.Buffered(1)),
            _layer_spec(l, (CONV_K, cdim), 1), _layer_spec(l, (1, cdim), 1),
            _layer_spec(l, (CONV_K, cdim), 1), _layer_spec(l, (1, cdim), 1),
        ],
        out_specs=[pl.BlockSpec((tm, n), lambda i: (i, 0)) for n in widths],
        out_shape=[jax.ShapeDtypeStruct((m, n), dt) for n, dt in zip(widths, dtypes)],
        scratch_shapes=[pltpu.VMEM((tm + CONV_PAD, cdim), F32), pltpu.VMEM((tm + CONV_PAD, cdim), F32)],
        compiler_params=_cp(("arbitrary",)),
        name="in_proj",
    )(h, norm_w, mod5, mod5, w_all, ssd_cw, ssd_cb, ml_cw, ml_cb)


def _ssd_kernel(p_ref, sm_ref, dtb_ref, alog_ref, dsk_ref, nw_ref, e8_ref, o_ref, st_ref):
    @pl.when(pl.program_id(0) == 0)
    def _():
        st_ref[...] = jnp.zeros(st_ref.shape, F32)

    nb = p_ref.shape[0]
    rng = range(nb)
    tri = _tri_mask()
    tri_bf = jnp.where(tri, 1.0, 0.0).astype(BF16)
    e8 = e8_ref[...]
    gw = SSD_GROUPS * SSD_STATE
    nh = SSD_HEADS // SSD_GROUPS
    hw = nh * SSD_HEAD_DIM
    lane = lax.broadcasted_iota(jnp.int32, (CHUNK, hw), 1)
    head_masks = [(lane >= hh * SSD_HEAD_DIM) & (lane < (hh + 1) * SSD_HEAD_DIM) for hh in range(nh)]

    neg_a = -jnp.exp(alog_ref[...])
    dt = [_softplus(sm_ref[b] + dtb_ref[...]) for b in rng]
    a_cum = _cumsum_rows_batched(tri_bf, [dt[b] * neg_a for b in rng])
    a_cum_t = [a_cum[b].T for b in rng]
    ex = _dot01_r2(jnp.concatenate([a_cum[b] for b in rng] + [dt[b] for b in rng], axis=0), e8)
    ac_e = [ex[b * CHUNK:(b + 1) * CHUNK] for b in rng]
    dt_e = [ex[(nb + b) * CHUNK:(nb + b + 1) * CHUNK] for b in rng]
    xs = [p_ref[b, :, D_BRANCH:2 * D_BRANCH].astype(F32) for b in rng]
    xd = [xs[b] * dt_e[b] for b in rng]
    xd_bf = [xd[b].astype(BF16) for b in rng]
    a_last_e = [ac_e[b][CHUNK - 1:CHUNK, :] for b in rng]
    xw_bf = [(xd[b] * jnp.exp(a_last_e[b] - ac_e[b])).astype(BF16) for b in rng]
    ea_e = [jnp.exp(ac_e[b]) for b in rng]

    ys = [[] for _ in rng]
    for g in range(SSD_GROUPS):
        gs = slice(g * hw, (g + 1) * hw)
        bg = [p_ref[b, :, 2 * D_BRANCH + g * SSD_STATE:2 * D_BRANCH + (g + 1) * SSD_STATE] for b in rng]
        cg = [p_ref[b, :, 2 * D_BRANCH + gw + g * SSD_STATE:2 * D_BRANCH + gw + (g + 1) * SSD_STATE]
              for b in rng]
        cb = [_dot_nt(cg[b], bg[b]) for b in rng]
        y_off = [_dot(cg[b], st_ref[b, g].astype(BF16)) for b in rng]
        m_cat, x_cat = [], []
        for b in rng:
            m_parts, x_parts = [], []
            for hh in range(nh):
                h = g * nh + hh
                seg = a_cum[b][:, h:h + 1] - a_cum_t[b][h:h + 1, :]
                dec = jnp.exp(jnp.where(tri, seg, NEG_BIG))
                m_parts.append((cb[b] * dec).astype(BF16))
                x_parts.append(jnp.where(head_masks[hh], xd_bf[b][:, gs], jnp.zeros((CHUNK, hw), BF16)))
            m_cat.append(jnp.concatenate(m_parts, axis=1))
            x_cat.append(jnp.concatenate(x_parts, axis=0))
        y_dg = [_dot(m_cat[b], x_cat[b]) for b in rng]
        s_loc = [_dot_tn(bg[b], xw_bf[b][:, gs]) for b in rng]
        for b in rng:
            ys[b].append(y_off[b] * ea_e[b][:, gs] + y_dg[b])
            st_ref[b, g] = st_ref[b, g] * jnp.exp(a_last_e[b][:, gs]) + s_loc[b]

    for b in rng:
        gate = p_ref[b, :, 0:D_BRANCH].astype(F32)
        y = jnp.concatenate(ys[b], axis=1) + dsk_ref[...] * xs[b]
        o_ref[b] = (_rms(y * gate) * nw_ref[...]).astype(o_ref.dtype)


def _ssd_call(l, p_ssd, p_sm, dt_bias, a_log, d_skip, norm_w, bsz, s):
    nc = s // CHUNK
    m = bsz * s
    e8 = np.zeros((LANES, D_BRANCH), np.float32)
    for h in range(SSD_HEADS):
        e8[h, h * SSD_HEAD_DIM:(h + 1) * SSD_HEAD_DIM] = 1.0
    row = lambda n: pl.BlockSpec((bsz, CHUNK, n), lambda c: (0, c, 0))
    return pl.pallas_call(
        _ssd_kernel,
        grid=(nc,),
        in_specs=[row(_W_SSD), row(LANES),
                  _layer_spec(l, (1, LANES), 2), _layer_spec(l, (1, LANES), 2),
                  _layer_spec(l, (1, D_BRANCH), 2), _layer_spec(l, (1, D_BRANCH), 2),
                  pl.BlockSpec((LANES, D_BRANCH), lambda c: (0, 0))],
        out_specs=row(D_BRANCH),
        out_shape=jax.ShapeDtypeStruct((bsz, s, D_BRANCH), BF16),
        scratch_shapes=[pltpu.VMEM((bsz, SSD_GROUPS, SSD_STATE, 256), F32)],
        compiler_params=_cp(("arbitrary",)),
        name="ssd_mixer",
    )(p_ssd.reshape(bsz, s, _W_SSD), p_sm.reshape(bsz, s, _W_SM), dt_bias, a_log, d_skip, norm_w,
      jnp.asarray(e8, BF16)).reshape(m, D_BRANCH)


def _mlstm_kernel(p_ref, sm_ref, ib_ref, fb_ref, nw_ref, jm_ref, o_ref, c_ref, m_ref):
    @pl.when(pl.program_id(0) == 0)
    def _():
        c_ref[...] = jnp.zeros(c_ref.shape, F32)
        m_ref[...] = jnp.zeros(m_ref.shape, F32)

    nb = p_ref.shape[0]
    rng = range(nb)
    d = D_BRANCH
    tri = _tri_mask()
    tri_bf = jnp.where(tri, 1.0, 0.0).astype(BF16)

    i_log = [sm_ref[b, :, LANES:2 * LANES] + ib_ref[...] for b in rng]
    f_log = [-_softplus(-(sm_ref[b, :, 2 * LANES:3 * LANES] + fb_ref[...])) for b in rng]
    f_cum = _cumsum_rows_batched(tri_bf, f_log)
    r = [i_log[b] - f_cum[b] for b in rng]
    r_t = [r[b].T for b in rng]
    a_col, inter, em, s_old, w_shift = [], [], [], [], []
    for b in rng:
        f_last = f_cum[b][CHUNK - 1:CHUNK, :]
        m_prev = m_ref[b]
        m_new = jnp.maximum(f_last + m_prev, jnp.max(f_last + r[b], axis=0, keepdims=True))
        s_old.append(jnp.exp(f_last + m_prev - m_new))
        w_shift.append(f_last - m_new)
        m_ref[b] = m_new
        g_inter = f_cum[b] + m_prev
        m_t = jnp.maximum(g_inter, f_cum[b] + _cummax_rows(r[b]))
        a_col.append(f_cum[b] - m_t)
        inter.append(jnp.exp(g_inter - m_t))
        em.append(jnp.exp(-m_t))

    ones = jnp.ones((CHUNK, ML_HEAD_DIM), BF16)
    ys = [[] for _ in rng]
    for h in range(ML_HEADS):
        c0 = h * ML_HEAD_DIM
        qh = [p_ref[b, :, d + c0:d + c0 + ML_HEAD_DIM] for b in rng]
        kh = [p_ref[b, :, 2 * d + c0:2 * d + c0 + ML_HEAD_DIM] for b in rng]
        v_ext = [jnp.concatenate([p_ref[b, :, 3 * d + c0:3 * d + c0 + ML_HEAD_DIM], ones], axis=1)
                 for b in rng]
        qk = [_dot_nt(qh[b], kh[b]) for b in rng]
        qc = [_dot(qh[b], c_ref[b, h].astype(BF16)) for b in rng]
        sm = []
        for b in rng:
            dmat = jnp.where(tri, a_col[b][:, h:h + 1] + r_t[b][h:h + 1, :], NEG_BIG)
            sm.append((qk[b] * jnp.exp(dmat)).astype(BF16))
        numden = [_dot(sm[b], v_ext[b]) + inter[b][:, h:h + 1] * qc[b] for b in rng]
        kw_t = [(kh[b].T.astype(F32) * jnp.exp(r_t[b][h:h + 1, :] + w_shift[b][:, h:h + 1])).astype(BF16)
                for b in rng]
        upd = [_dot(kw_t[b], v_ext[b]) for b in rng]
        for b in rng:
            num = numden[b][:, 0:ML_HEAD_DIM]
            den = numden[b][:, ML_HEAD_DIM:]
            o_gate = p_ref[b, :, 4 * d + c0:4 * d + c0 + ML_HEAD_DIM].astype(F32)
            ys[b].append(o_gate * (num / jnp.maximum(jnp.abs(den), em[b][:, h:h + 1])))
            c_ref[b, h] = s_old[b][:, h:h + 1] * c_ref[b, h] + upd[b]

    y = [jnp.concatenate(ys[b], axis=1) for b in rng]
    ms = [_dot((y[b] * y[b]).astype(BF16), jm_ref[...]) for b in rng]
    for b in rng:
        gate = p_ref[b, :, 0:d].astype(F32)
        o_ref[b] = (y[b] * lax.rsqrt(ms[b] + NORM_EPS) * nw_ref[...] * gate).astype(o_ref.dtype)


def _mlstm_call(l, p_ml, p_sm, i_bias, f_bias, norm_w, bsz, s):
    nc = s // CHUNK
    m = bsz * s
    row = lambda n: pl.BlockSpec((bsz, CHUNK, n), lambda c: (0, c, 0))
    jm = np.kron(np.eye(ML_HEADS), np.full((ML_HEAD_DIM, ML_HEAD_DIM), 1.0 / ML_HEAD_DIM))
    return pl.pallas_call(
        _mlstm_kernel,
        grid=(nc,),
        in_specs=[row(_W_ML), row(_W_SM),
                  _layer_spec(l, (1, LANES), 2), _layer_spec(l, (1, LANES), 2),
                  _layer_spec(l, (1, D_BRANCH), 2),
                  pl.BlockSpec((D_BRANCH, D_BRANCH), lambda c: (0, 0))],
        out_specs=row(D_BRANCH),
        out_shape=jax.ShapeDtypeStruct((bsz, s, D_BRANCH), BF16),
        scratch_shapes=[pltpu.VMEM((bsz, ML_HEADS, ML_HEAD_DIM, 2 * ML_HEAD_DIM), F32),
                        pltpu.VMEM((bsz, 1, LANES), F32)],
        compiler_params=_cp(("arbitrary",)),
        name="mlstm_mixer",
    )(p_ml.reshape(bsz, s, _W_ML), p_sm.reshape(bsz, s, _W_SM), i_bias, f_bias,
      norm_w, jnp.asarray(jm, BF16)).reshape(m, D_BRANCH)


def _ret_log_gamma():
    return [math.log1p(-2.0 ** (-(RET_DECAY_BASE + h))) for h in range(RET_HEADS)]


def _ret_consts():
    lg = _ret_log_gamma()
    pos = np.arange(CHUNK, dtype=np.float64)
    rel = pos[:, None] - pos[None, :]
    dm = np.stack([np.where(rel >= 0, np.exp(np.maximum(rel, 0.0) * g), 0.0) for g in lg])
    from_start = np.concatenate(
        [np.repeat(np.exp((pos + 1.0) * g)[:, None], RET_V, axis=1) for g in lg], axis=1)
    to_end = np.concatenate(
        [np.repeat(np.exp((CHUNK - 1.0 - pos) * g)[:, None], RET_QK, axis=1) for g in lg], axis=1)
    qw, vw = RET_HEADS * RET_QK, RET_HEADS * RET_V
    chunk_decay = np.zeros((qw, vw))
    for h, g in enumerate(lg):
        chunk_decay[h * RET_QK:(h + 1) * RET_QK, h * RET_V:(h + 1) * RET_V] = math.exp(CHUNK * g)
    block_diag = (chunk_decay > 0).astype(np.float64)
    swap = np.zeros((qw, qw))
    for j in range(qw):
        base, off = (j // RET_QK) * RET_QK, j % RET_QK
        swap[base + (off + RET_QK // 2) % RET_QK, j] = 1.0
    f = lambda a: jnp.asarray(a, F32)
    return f(dm), f(from_start), f(to_end), f(chunk_decay), f(block_diag), jnp.asarray(swap, BF16)


def _ret_kernel(p_ref, cos_ref, sin_ref, dm_ref, fs_ref, te_ref, cd_ref, bd_ref, sw_ref, nw_ref,
                o_ref, r_ref):
    @pl.when(pl.program_id(0) == 0)
    def _():
        r_ref[...] = jnp.zeros(r_ref.shape, F32)

    nb = p_ref.shape[0]
    d = D_BRANCH
    qw = RET_HEADS * RET_QK
    sw = sw_ref[...]
    lane = lax.broadcasted_iota(jnp.int32, (CHUNK, qw), 1)
    head_masks = [(lane >= h * RET_QK) & (lane < (h + 1) * RET_QK) for h in range(RET_HEADS)]

    def v_of(b):
        return p_ref[b, :, d + 2 * qw:2 * d + 2 * qw]

    q_raw = [p_ref[b, :, d:d + qw] for b in range(nb)]
    k_raw = [p_ref[b, :, d + qw:d + 2 * qw] for b in range(nb)]
    q_sw = [_dot(q_raw[b], sw) for b in range(nb)]
    k_sw = [_dot(k_raw[b], sw) for b in range(nb)]
    qs, ks = [], []
    for b in range(nb):
        cos_e = jnp.concatenate([cos_ref[b]] * (qw // LANES), axis=1)
        sin_e = jnp.concatenate([sin_ref[b]] * (qw // LANES), axis=1)
        qs.append(q_raw[b].astype(F32) * cos_e + q_sw[b] * sin_e)
        ks.append((k_raw[b].astype(F32) * cos_e + k_sw[b] * sin_e) * (RET_QK ** -0.5))
    k_bf = [k.astype(BF16) for k in ks]
    ys = [_dot(qs[b].astype(BF16), r_ref[b].astype(BF16)) * fs_ref[...] for b in range(nb)]
    for h in range(RET_HEADS):
        scores = [(_dot_nt(jnp.where(head_masks[h], qs[b], jnp.zeros_like(qs[b])).astype(BF16), k_bf[b])
                   * dm_ref[h]).astype(BF16) for b in range(nb)]
        inner = [_dot(scores[b], v_of(b)[:, h * RET_V:(h + 1) * RET_V]) for b in range(nb)]
        pad_l, pad_r = h * RET_V, (RET_HEADS - 1 - h) * RET_V
        for b in range(nb):
            parts = ([jnp.zeros((CHUNK, pad_l), F32)] if pad_l else []) + [inner[b]] + \
                    ([jnp.zeros((CHUNK, pad_r), F32)] if pad_r else [])
            ys[b] = ys[b] + jnp.concatenate(parts, axis=1)
    upd = [_dot_tn((ks[b] * te_ref[...]).astype(BF16), v_of(b)) for b in range(nb)]
    for b in range(nb):
        r_ref[b] = r_ref[b] * cd_ref[...] + upd[b] * bd_ref[...]
    for b in range(nb):
        gate = p_ref[b, :, 0:d].astype(F32)
        outs = [_rms(ys[b][:, h * RET_V:(h + 1) * RET_V]) for h in range(RET_HEADS)]
        o_ref[b] = (jnp.concatenate(outs, axis=1) * nw_ref[...] * gate).astype(o_ref.dtype)


def _ret_call(l, p_ret, cos_t, sin_t, consts, norm_w, bsz, s):
    nc = s // CHUNK
    m = bsz * s
    qw, vw = RET_HEADS * RET_QK, RET_HEADS * RET_V
    dm, fs, te, cd, bd, sw = consts
    const = lambda shape: pl.BlockSpec(shape, lambda c: (0,) * len(shape))
    row = lambda n: pl.BlockSpec((bsz, CHUNK, n), lambda c: (0, c, 0))
    return pl.pallas_call(
        _ret_kernel,
        grid=(nc,),
        in_specs=[row(_W_RET), row(LANES), row(LANES),
                  const((RET_HEADS, CHUNK, CHUNK)), const((CHUNK, vw)), const((CHUNK, qw)),
                  const((qw, vw)), const((qw, vw)), const((qw, qw)),
                  _layer_spec(l, (1, D_BRANCH), 2)],
        out_specs=row(D_BRANCH),
        out_shape=jax.ShapeDtypeStruct((bsz, s, D_BRANCH), BF16),
        scratch_shapes=[pltpu.VMEM((bsz, qw, vw), F32)],
        compiler_params=_cp(("arbitrary",)),
        name="retention_mixer",
    )(p_ret.reshape(bsz, s, _W_RET), cos_t.reshape(bsz, s, LANES), sin_t.reshape(bsz, s, LANES),
      dm, fs, te, cd, bd, sw, norm_w).reshape(m, D_BRANCH)


def _s5_compact(lam_re, lam_im, b_re, b_im, c_re, c_im, log_step):
    depth = lam_re.shape[0]
    nb, ngl, sub = S5_LANE_BLOCKS, S5_LOCAL_GROUPS, S5_SUB
    step = jnp.exp(log_step.astype(F32))[..., None]
    lr = jnp.minimum(lam_re.astype(F32), -1e-4)
    li = lam_im.astype(F32)
    mag = jnp.exp(lr * step)
    ang = li * step
    ab_re = mag * jnp.cos(ang)
    ab_im = mag * jnp.sin(ang)
    den = lr * lr + li * li
    coef_re = ((ab_re - 1.0) * lr + ab_im * li) / den
    coef_im = (ab_im * lr - (ab_re - 1.0) * li) / den
    br_t, bi_t = jnp.swapaxes(b_re.astype(F32), -1, -2), jnp.swapaxes(b_im.astype(F32), -1, -2)
    cf_re, cf_im = coef_re[:, :, None, :], coef_im[:, :, None, :]
    bb_re = cf_re * br_t - cf_im * bi_t
    bb_im = cf_re * bi_t + cf_im * br_t
    pw_re, pw_im = [jnp.ones_like(ab_re)], [jnp.zeros_like(ab_im)]
    for _ in range(sub):
        pr, pi = pw_re[-1], pw_im[-1]
        pw_re.append(pr * ab_re - pi * ab_im)
        pw_im.append(pr * ab_im + pi * ab_re)
    cr, ci = c_re.astype(F32), c_im.astype(F32)
    cr_t, ci_t = jnp.swapaxes(cr, -1, -2), jnp.swapaxes(ci, -1, -2)
    hp = lax.Precision.HIGHEST
    abb_re = [pw_re[k][:, :, None, :] * bb_re - pw_im[k][:, :, None, :] * bb_im for k in range(sub)]
    abb_im = [pw_re[k][:, :, None, :] * bb_im + pw_im[k][:, :, None, :] * bb_re for k in range(sub)]
    taps_t = [jnp.einsum('lgip,lgop->lgio', abb_re[k], cr, precision=hp)
              - jnp.einsum('lgip,lgop->lgio', abb_im[k], ci, precision=hp) for k in range(sub)]

    def by_block(rows):
        a = jnp.stack(rows).reshape(len(rows), depth, nb, ngl, rows[0].shape[-2], rows[0].shape[-1])
        return a.transpose(1, 2, 0, 3, 4, 5).reshape(depth, nb, -1, rows[0].shape[-1])

    c_in = jnp.concatenate([by_block([abb_re[sub - 1 - t] for t in range(sub)]),
                            by_block([abb_im[sub - 1 - t] for t in range(sub)])], axis=-1)
    zero = jnp.zeros_like(taps_t[0])
    toe = by_block([jnp.concatenate([zero] * t + taps_t[:sub - t], axis=-1) for t in range(sub)])

    def rows_x(c_t, s_t, sign):
        cols = [c_t * pw_re[k][..., None] - sign * s_t * pw_im[k][..., None] for k in range(1, sub + 1)]
        return jnp.stack(cols, axis=-2).reshape(depth, nb, ngl * S5_STATE, sub * S5_GROUP)

    x_re = rows_x(cr_t, ci_t, 1.0)
    x_im = -(jnp.stack([cr_t * pw_im[k][..., None] + ci_t * pw_re[k][..., None]
                        for k in range(1, sub + 1)], axis=-2)
             .reshape(depth, nb, ngl * S5_STATE, sub * S5_GROUP))
    c_out = jnp.concatenate([toe, x_re, x_im], axis=2)
    a8 = jnp.concatenate([pw_re[sub].reshape(depth, nb, 1, ngl * S5_STATE),
                          pw_im[sub].reshape(depth, nb, 1, ngl * S5_STATE)], axis=-1)
    return c_in.astype(BF16), c_out.astype(BF16), a8


def _s5_expanders():
    kin = S5_SUB * LANES
    half = S5_LOCAL_GROUPS * S5_STATE
    sel_in = np.zeros((LANES, 2 * half), np.float32)
    csel_in = np.zeros((LANES, 2 * half), np.float32)
    for col in range(2 * half):
        sel_in[(col // half) * S5_STATE + col % S5_STATE, col] = 1.0
        csel_in[(col % half) // S5_STATE, col] = 1.0
    sel_out = np.zeros((LANES, kin), np.float32)
    csel_out = np.zeros((LANES, kin), np.float32)
    for col in range(kin):
        sel_out[(col // LANES) * S5_GROUP + col % S5_GROUP, col] = 1.0
        csel_out[(col // S5_GROUP) % S5_LOCAL_GROUPS, col] = 1.0
    rsel_in = np.zeros((kin, LANES), np.float32)
    for r in range(kin):
        rsel_in[r, (r // S5_GROUP) % S5_LOCAL_GROUPS] = 1.0
    rsel_out = np.zeros((2 * kin, LANES), np.float32)
    rsel_out[:kin] = rsel_in
    for r in range(kin):
        rsel_out[kin + r, (r % half) // S5_STATE] = 1.0
    return tuple(jnp.asarray(a, BF16) for a in (sel_in, csel_in, rsel_in, sel_out, csel_out, rsel_out))


def _s5_fold(a, bsz, s):
    j = s // S5_SUB
    a = a.reshape(bsz, j, S5_SUB, S5_LANE_BLOCKS, LANES).transpose(1, 0, 3, 2, 4)
    return a.reshape(j * bsz, S5_LANE_BLOCKS * S5_SUB * LANES)


def _s5_unfold(a, bsz, s):
    j = s // S5_SUB
    a = a.reshape(j, bsz, S5_LANE_BLOCKS, S5_SUB, LANES).transpose(1, 0, 3, 2, 4)
    return a.reshape(bsz * s, D_BRANCH)


S5_ROW_BLOCK = 512


def _s5_scan_kernel(u_ref, cin_ref, cout_ref, a8_ref, sin_ref, csin_ref, rsin_ref,
                    sout_ref, csout_ref, rsout_ref, o_ref, x_ref, min_ref, mout_ref, *, bsz):
    rows = u_ref.shape[0]
    half = S5_LOCAL_GROUPS * S5_STATE
    rb = min(S5_ROW_BLOCK, rows)

    min_ref[...] = (_dot(cin_ref[...], sin_ref[...]) * _dot(rsin_ref[...], csin_ref[...])).astype(BF16)
    mout_ref[...] = (_dot(cout_ref[...], sout_ref[...]) * _dot(rsout_ref[...], csout_ref[...])).astype(BF16)

    for r0 in range(0, rows, rb):
        x_ref[r0:r0 + rb, :] = _dot(u_ref[r0:r0 + rb, :], min_ref[...])

    a_re = a8_ref[:, 0:half]
    a_im = a8_ref[:, half:]

    def body(i, carry):
        x_re, x_im = carry
        r0 = pl.multiple_of(i * bsz, bsz)
        loc = x_ref[pl.ds(r0, bsz), :]
        x_ref[pl.ds(r0, bsz), :] = jnp.concatenate([x_re, x_im], axis=1)
        n_re = a_re * x_re - a_im * x_im + loc[:, 0:half]
        n_im = a_re * x_im + a_im * x_re + loc[:, half:]
        return n_re, n_im

    zero = jnp.zeros((bsz, half), F32)
    lax.fori_loop(0, rows // bsz, body, (zero, zero), unroll=8)

    for r0 in range(0, rows, rb):
        lhs = jnp.concatenate([u_ref[r0:r0 + rb, :], x_ref[r0:r0 + rb, :].astype(BF16)], axis=1)
        o_ref[r0:r0 + rb, :] = _dot(lhs, mout_ref[...]).astype(o_ref.dtype)


def _s5_scan_call(l, u_fold, c_in, c_out, a8, expanders, bsz):
    rows = u_fold.shape[0]
    kin = S5_SUB * LANES
    lspec = lambda shape: pl.BlockSpec((None, None) + shape, lambda lb: (l, lb, 0, 0))
    const = lambda a: pl.BlockSpec(a.shape, lambda lb: (0, 0))
    sel_in, csel_in, rsel_in, sel_out, csel_out, rsel_out = expanders
    return pl.pallas_call(
        functools.partial(_s5_scan_kernel, bsz=bsz),
        grid=(S5_LANE_BLOCKS,),
        in_specs=[pl.BlockSpec((rows, kin), lambda lb: (0, lb)),
                  lspec((kin, LANES)), lspec((2 * kin, LANES)), lspec((1, kin)),
                  const(sel_in), const(csel_in), const(rsel_in),
                  const(sel_out), const(csel_out), const(rsel_out)],
        out_specs=pl.BlockSpec((rows, kin), lambda lb: (0, lb)),
        out_shape=jax.ShapeDtypeStruct(u_fold.shape, BF16),
        scratch_shapes=[pltpu.VMEM((rows, kin), F32),
                        pltpu.VMEM((kin, kin), BF16), pltpu.VMEM((2 * kin, kin), BF16)],
        compiler_params=_cp(("parallel",)),
        name="s5_scan",
    )(u_fold, c_in, c_out, a8, sel_in, csel_in, rsel_in, sel_out, csel_out, rsel_out)


def _s5_post_kernel(g_ref, u_ref, y_ref, d_ref, wg_ref, bg_ref, nw_ref, o_ref):
    d = D_BRANCH
    gate = g_ref[...].astype(F32)
    u = u_ref[...].astype(F32)
    y = y_ref[...].astype(F32) + d_ref[...] * u
    y = 0.5 * y * (1.0 + jnp.tanh(math.sqrt(2.0 / math.pi) * (y + 0.044715 * (y * y * y))))
    g = _dot(y.astype(BF16), wg_ref[...]) + bg_ref[...]
    y = g[:, 0:d] * _sigmoid(g[:, d:2 * d])
    o_ref[...] = (_rms(y) * nw_ref[...] * gate).astype(o_ref.dtype)


def _s5_post_call(l, p_gate, p_u, y_ssm, d_skip, w_glu, b_glu, norm_w):
    m = p_u.shape[0]
    tm = min(ROW_TILE, m)
    return pl.pallas_call(
        _s5_post_kernel,
        grid=(m // tm,),
        in_specs=[pl.BlockSpec((tm, D_BRANCH), lambda i: (i, 0)),
                  pl.BlockSpec((tm, D_BRANCH), lambda i: (i, 0)),
                  pl.BlockSpec((tm, D_BRANCH), lambda i: (i, 0)),
                  _layer_spec(l, (1, D_BRANCH), 1), _layer_spec(l, (D_BRANCH, 2 * D_BRANCH), 1),
                  _layer_spec(l, (1, 2 * D_BRANCH), 1), _layer_spec(l, (1, D_BRANCH), 1)],
        out_specs=pl.BlockSpec((tm, D_BRANCH), lambda i: (i, 0)),
        out_shape=jax.ShapeDtypeStruct((m, D_BRANCH), BF16),
        compiler_params=_cp(("parallel",)),
        name="s5_post",
    )(p_gate, p_u, y_ssm, d_skip, w_glu, b_glu, norm_w)


def _outproj_kernel(y1, y2, y3, y4, w_ref, h_ref, g_ref, fw_ref, o_ref, *, final_norm):
    acc = _dot(y1[...], w_ref[0 * D_BRANCH:1 * D_BRANCH, :])
    acc = acc + _dot(y2[...], w_ref[1 * D_BRANCH:2 * D_BRANCH, :])
    acc = acc + _dot(y3[...], w_ref[2 * D_BRANCH:3 * D_BRANCH, :])
    acc = acc + _dot(y4[...], w_ref[3 * D_BRANCH:4 * D_BRANCH, :])
    h = h_ref[...] + g_ref[...] * acc
    if final_norm:
        h = _rms(h) * fw_ref[...]
    o_ref[...] = h


def _outproj_call(l, ys, w_out, h, mod5, final_w, s, final_norm):
    m = h.shape[0]
    tm = min(ROW_TILE, s)
    per_batch = s // tm
    yspec = pl.BlockSpec((tm, D_BRANCH), lambda i: (i, 0))
    return pl.pallas_call(
        functools.partial(_outproj_kernel, final_norm=final_norm),
        grid=(m // tm,),
        in_specs=[yspec] * 4 + [
            _layer_spec(l, (4 * D_BRANCH, D_MODEL), 1),
            pl.BlockSpec((tm, D_MODEL), lambda i: (i, 0)),
            pl.BlockSpec((None, None, None, 1, D_MODEL), lambda i: (l, i // per_batch, 2, 0, 0)),
            pl.BlockSpec((1, D_MODEL), lambda i: (0, 0)),
        ],
        out_specs=pl.BlockSpec((tm, D_MODEL), lambda i: (i, 0)),
        out_shape=jax.ShapeDtypeStruct((m, D_MODEL), F32),
        compiler_params=_cp(("parallel",)),
        name="out_proj",
    )(*ys, w_out, h, mod5, final_w)


_PACK_PIECES = ((0, _O_SSD_Z, _W_SSD), (_W_SSD, _O_ML_Z, _W_ML), (_W_SSD + _W_ML, _O_S5_Z, _W_S5),
                (_W_SSD + _W_ML + _W_S5, _O_RET_Z, _W_RET),
                (_W_ALL - _W_SM, _O_SSD_DT, _O_ML_Z - _O_SSD_DT),
                (_W_ALL - _W_SM + LANES, _O_ML_I, _O_ML_F - _O_ML_I),
                (_W_ALL - _W_SM + 2 * LANES, _O_ML_F, _O_S5_Z - _O_ML_F))


def _pack_kernel(w_ref, o_ref):
    o_ref[:, _W_ALL - _W_SM:_W_ALL] = jnp.zeros((o_ref.shape[0], _W_SM), BF16)
    for dst, src, width in _PACK_PIECES:
        o_ref[:, dst:dst + width] = w_ref[:, src:src + width].astype(BF16)


def _pack_w_in(w_in):
    depth, d, n = w_in.shape
    assert n == _D_IN
    tr = 256
    return pl.pallas_call(
        _pack_kernel,
        grid=(depth, d // tr),
        in_specs=[pl.BlockSpec((None, tr, n), lambda l, i: (l, i, 0))],
        out_specs=pl.BlockSpec((None, tr, _W_ALL), lambda l, i: (l, i, 0)),
        out_shape=jax.ShapeDtypeStruct((depth, d, _W_ALL), BF16),
        compiler_params=_cp(("parallel", "parallel")),
        name="pack_w_in",
    )(w_in)


def _pad_lanes(v):
    return jnp.pad(v.astype(F32), ((0, 0), (0, LANES - v.shape[-1])))[:, None, :]


def kernel(x, c, positions, norm_w, w_ada, b_ada, w_in, w_out, ssd_conv_w, ssd_conv_b, ssd_dt_bias, ssd_a_log, ssd_d, ssd_norm_w, ml_conv_w, ml_conv_b, ml_i_bias, ml_f_bias, ml_norm_w, s5_lambda_re, s5_lambda_im, s5_b_re, s5_b_im, s5_c_re, s5_c_im, s5_d, s5_log_step, s5_w_glu, s5_b_glu, s5_norm_w, ret_norm_w, final_norm_w):
    bsz, s, d = x.shape
    depth = w_in.shape[0]
    assert d == D_MODEL and s % CHUNK == 0 and bsz % 8 == 0
    m = bsz * s
    row3 = lambda a: a.astype(F32)[:, None, :]

    mod = _mod_call(c.astype(F32), w_ada, b_ada)
    mod5 = mod.reshape(depth, bsz, 3, 1, D_MODEL)
    cos_t, sin_t = _rope_call(positions)
    ret_consts = _ret_consts()
    w_all = _pack_w_in(w_in)
    w_out_bf = w_out.astype(BF16)
    w_glu_bf = s5_w_glu.astype(BF16)
    c_in, c_out, a8 = _s5_compact(s5_lambda_re, s5_lambda_im, s5_b_re, s5_b_im, s5_c_re, s5_c_im,
                                  s5_log_step)
    s5_exp = _s5_expanders()
    norm_w3, final_w = row3(norm_w), final_norm_w.reshape(1, D_MODEL).astype(F32)
    ssd_cb3, ml_cb3 = row3(ssd_conv_b), row3(ml_conv_b)
    dt_bias3, a_log3 = _pad_lanes(ssd_dt_bias), _pad_lanes(ssd_a_log)
    d_skip3 = row3(jnp.repeat(ssd_d, SSD_HEAD_DIM, axis=-1))
    i_bias3, f_bias3 = _pad_lanes(ml_i_bias), _pad_lanes(ml_f_bias)
    ssd_nw3, ml_nw3, s5_nw3, ret_nw3 = row3(ssd_norm_w), row3(ml_norm_w), row3(s5_norm_w), row3(ret_norm_w)
    s5_d3, b_glu3 = row3(s5_d), row3(s5_b_glu)

    h = x.reshape(m, D_MODEL)
    for l in range(depth):
        p_ssd, p_ml, p_s5g, p_s5u, p_ret, p_sm = _inproj_call(
            l, h, norm_w3, mod5, w_all, ssd_conv_w, ssd_cb3, ml_conv_w, ml_cb3, s)
        y_ssd = _ssd_call(l, p_ssd, p_sm, dt_bias3, a_log3, d_skip3, ssd_nw3, bsz, s)
        y_ml = _mlstm_call(l, p_ml, p_sm, i_bias3, f_bias3, ml_nw3, bsz, s)
        y_fold = _s5_scan_call(l, _s5_fold(p_s5u, bsz, s), c_in, c_out, a8, s5_exp, bsz)
        y_s5 = _s5_post_call(l, p_s5g, p_s5u, _s5_unfold(y_fold, bsz, s), s5_d3, w_glu_bf, b_glu3, s5_nw3)
        y_ret = _ret_call(l, p_ret, cos_t, sin_t, ret_consts, ret_nw3, bsz, s)
        h = _outproj_call(l, (y_ssd, y_ml, y_s5, y_ret), w_out_bf, h, mod5, final_w, s,
                          final_norm=(l == depth - 1))
    return h.reshape(bsz, s, D_MODEL)
```

```python
import functools
import math

import numpy as np
import jax
import jax.numpy as jnp
from jax import lax
from jax.experimental import pallas as pl
from jax.experimental.pallas import tpu as pltpu

F32 = jnp.float32
BF16 = jnp.bfloat16

D_MODEL = 1024
D_BRANCH = 512
CHUNK = 128
CONV_K = 4
CONV_PAD = 8
NORM_EPS = 1e-6
NEG_BIG = -1e30

SSD_HEADS = 8
SSD_HEAD_DIM = 64
SSD_GROUPS = 2
SSD_STATE = 128
ML_HEADS = 4
ML_HEAD_DIM = 128
S5_GROUPS = 32
S5_GROUP = 16
S5_STATE = 64
S5_SUB = 8
S5_LANE_BLOCKS = 4
S5_LOCAL_GROUPS = S5_GROUPS // S5_LANE_BLOCKS
RET_HEADS = 4
RET_QK = 64
RET_V = 128
RET_DECAY_BASE = 5.0
ROPE_BASE = 10000.0
LANES = 128

_O_SSD_Z, _O_SSD_DT = 0, 1536
_O_ML_Z, _O_ML_I, _O_ML_F = 1544, 4104, 4108
_O_S5_Z = 4112
_O_RET_Z = 5136
_D_IN = 6672
_W_SSD, _W_ML, _W_S5, _W_RET, _W_SM = 1536, 2560, 1024, 1536, 384
_W_ALL = _W_SSD + _W_ML + _W_S5 + _W_RET + _W_SM
_W_S5G = _W_S5 - D_BRANCH

VMEM_LIMIT = 56 * 1024 * 1024
ROW_TILE = 512


def _cp(sem):
    return pltpu.CompilerParams(dimension_semantics=sem, vmem_limit_bytes=VMEM_LIMIT)


def _layer_spec(l, shape, nargs):
    return pl.BlockSpec((None,) + tuple(shape), lambda *_: (l,) + (0,) * len(shape))


def _dot(a, b):
    return jnp.dot(a, b, preferred_element_type=F32)


def _dot_nt(a, b):
    return lax.dot_general(a, b, (((1,), (1,)), ((), ())), preferred_element_type=F32)


def _dot_tn(a, b):
    return lax.dot_general(a, b, (((0,), (0,)), ((), ())), preferred_element_type=F32)


def _sigmoid(x):
    return 0.5 + 0.5 * jnp.tanh(0.5 * x)


def _silu(x):
    hx = 0.5 * x
    return hx + hx * jnp.tanh(hx)


def _softplus(x):
    return jnp.maximum(x, 0.0) + jnp.log1p(jnp.exp(-jnp.abs(x)))


def _split3(v):
    hi = v.astype(BF16)
    r1 = v - hi.astype(F32)
    mid = r1.astype(BF16)
    lo = (r1 - mid.astype(F32)).astype(BF16)
    return hi, mid, lo


def _dot01_r2(v, m01):
    hi = v.astype(BF16)
    lo = (v - hi.astype(F32)).astype(BF16)
    return _dot(hi, m01) + _dot(lo, m01)


def _cumsum_rows_batched(tri_bf, xs):
    terms = []
    for x in xs:
        terms.extend(_split3(x))
    res = _dot(tri_bf, jnp.concatenate(terms, axis=1))
    w = xs[0].shape[1]
    return [res[:, (3 * i) * w:(3 * i + 1) * w] + res[:, (3 * i + 1) * w:(3 * i + 2) * w]
            + res[:, (3 * i + 2) * w:(3 * i + 3) * w] for i in range(len(xs))]


def _cummax_rows(x):
    n = x.shape[0]
    row = lax.broadcasted_iota(jnp.int32, x.shape, 0)
    sh = 1
    while sh < n:
        x = jnp.maximum(x, jnp.where(row >= sh, pltpu.roll(x, sh, axis=0), NEG_BIG))
        sh *= 2
    return x


def _tri_mask():
    row = lax.broadcasted_iota(jnp.int32, (CHUNK, CHUNK), 0)
    col = lax.broadcasted_iota(jnp.int32, (CHUNK, CHUNK), 1)
    return row >= col


def _rms(y):
    return y * lax.rsqrt(jnp.mean(y * y, axis=-1, keepdims=True) + NORM_EPS)


def _mod_kernel(c_ref, w_ref, b_ref, o_ref):
    cond = _silu(c_ref[...])
    o_ref[0] = _dot(cond.astype(BF16), w_ref[0].astype(BF16)) + b_ref[0]


def _mod_call(c, w_ada, b_ada):
    depth, d, n = w_ada.shape
    bsz = c.shape[0]
    tn = 1024
    return pl.pallas_call(
        _mod_kernel,
        grid=(depth, n // tn),
        in_specs=[
            pl.BlockSpec((bsz, d), lambda l, j: (0, 0)),
            pl.BlockSpec((1, d, tn), lambda l, j: (l, 0, j)),
            pl.BlockSpec((1, 1, tn), lambda l, j: (l, 0, j)),
        ],
        out_specs=pl.BlockSpec((1, bsz, tn), lambda l, j: (l, 0, j)),
        out_shape=jax.ShapeDtypeStruct((depth, bsz, n), F32),
        compiler_params=_cp(("parallel", "parallel")),
        name="adaln_mod",
    )(c, w_ada, b_ada.reshape(depth, 1, n))


def _rope_kernel(pos_ref, f_ref, sg_ref, cos_ref, sin_ref):
    ang = pos_ref[...] * f_ref[...]
    cos_ref[...] = jnp.cos(ang)
    sin_ref[...] = jnp.sin(ang) * sg_ref[...]


def _rope_call(positions):
    bsz, s = positions.shape
    m = bsz * s
    half = RET_QK // 2
    inv_freq = jnp.exp(-math.log(ROPE_BASE) * jnp.arange(half, dtype=F32) / half)
    f128 = jnp.tile(inv_freq, LANES // half).reshape(1, LANES)
    sign = np.tile(np.concatenate([-np.ones(half), np.ones(half)]), LANES // RET_QK)
    sign = jnp.asarray(sign.reshape(1, LANES), F32)
    pos = jnp.broadcast_to(positions.astype(F32).reshape(m, 1), (m, LANES))
    tm = min(1024, m)
    return pl.pallas_call(
        _rope_kernel,
        grid=(m // tm,),
        in_specs=[
            pl.BlockSpec((tm, LANES), lambda i: (i, 0)),
            pl.BlockSpec((1, LANES), lambda i: (0, 0)),
            pl.BlockSpec((1, LANES), lambda i: (0, 0)),
        ],
        out_specs=[pl.BlockSpec((tm, LANES), lambda i: (i, 0))] * 2,
        out_shape=[jax.ShapeDtypeStruct((m, LANES), F32)] * 2,
        compiler_params=_cp(("parallel",)),
        name="rope_tables",
    )(pos, f128, sign)


def _conv_silu(x, buf_ref, cw_ref, cb_ref):
    n = x.shape[0]
    buf_ref[CONV_PAD:CONV_PAD + n, :] = x
    acc = cb_ref[...] + cw_ref[CONV_K - 1:CONV_K, :] * x
    for k in range(CONV_K - 1):
        off = CONV_PAD - (CONV_K - 1) + k
        acc = acc + cw_ref[k:k + 1, :] * buf_ref[off:off + n, :]
    buf_ref[0:CONV_PAD, :] = x[n - CONV_PAD:n, :]
    return _silu(acc)


def _inproj_kernel(h_ref, nw_ref, sc_ref, sh_ref, w_ref, scw_ref, scb_ref, mcw_ref, mcb_ref,
                   o_ssd, o_ml, o_s5g, o_s5u, o_ret, o_sm, sbuf_ref, mbuf_ref, *, per_batch):
    @pl.when(pl.program_id(0) % per_batch == 0)
    def _():
        sbuf_ref[0:CONV_PAD, :] = jnp.zeros((CONV_PAD, sbuf_ref.shape[1]), F32)
        mbuf_ref[0:CONV_PAD, :] = jnp.zeros((CONV_PAD, mbuf_ref.shape[1]), F32)

    x = h_ref[...]
    hn = _rms(x) * nw_ref[...]
    hb = (hn * (1.0 + sc_ref[...]) + sh_ref[...]).astype(BF16)
    d = D_BRANCH

    def proj(col, width):
        return _dot(hb, w_ref[:, col:col + width])

    col = 0
    o_ssd[:, 0:d] = _silu(proj(col, d)).astype(BF16)
    o_ssd[:, d:3 * d] = _conv_silu(proj(col + d, 2 * d), sbuf_ref, scw_ref, scb_ref).astype(BF16)
    col += _W_SSD
    o_ml[:, 0:d] = _silu(proj(col, d)).astype(BF16)
    qk = _conv_silu(proj(col + d, 2 * d), mbuf_ref, mcw_ref, mcb_ref)
    o_ml[:, d:2 * d] = qk[:, 0:d].astype(BF16)
    o_ml[:, 2 * d:3 * d] = (qk[:, d:2 * d] * (ML_HEAD_DIM ** -0.5)).astype(BF16)
    o_ml[:, 3 * d:4 * d] = proj(col + 3 * d, d).astype(BF16)
    o_ml[:, 4 * d:5 * d] = _sigmoid(proj(col + 4 * d, d)).astype(BF16)
    col += _W_ML
    o_s5g[...] = _silu(proj(col, d)).astype(BF16)
    o_s5u[...] = proj(col + d, d).astype(BF16)
    col += _W_S5
    o_ret[:, 0:d] = _silu(proj(col, d)).astype(BF16)
    o_ret[:, d:3 * d] = proj(col + d, 2 * d).astype(BF16)
    col += _W_RET
    o_sm[...] = proj(col, _W_SM)


def _inproj_call(l, h, norm_w, mod5, w_all, ssd_cw, ssd_cb, ml_cw, ml_cb, s):
    m = h.shape[0]
    tm = min(ROW_TILE, s)
    per_batch = s // tm
    widths = (_W_SSD, _W_ML, _W_S5G, D_BRANCH, _W_RET, _W_SM)
    dtypes = (BF16, BF16, BF16, BF16, BF16, F32)
    cdim = 2 * D_BRANCH
    mod_spec = lambda which: pl.BlockSpec((None, None, None, 1, D_MODEL),
                                          lambda i: (l, i // per_batch, which, 0, 0))
    return pl.pallas_call(
        functools.partial(_inproj_kernel, per_batch=per_batch),
        grid=(m // tm,),
        in_specs=[
            pl.BlockSpec((tm, D_MODEL), lambda i: (i, 0)),
            _layer_spec(l, (1, D_MODEL), 1),
            mod_spec(1), mod_spec(0),
            pl.BlockSpec((None, D_MODEL, _W_ALL), lambda i: (l, 0, 0), pipeline_mode=pl.Buffered(1)),
            _layer_spec(l, (CONV_K, cdim), 1), _layer_spec(l, (1, cdim), 1),
            _layer_spec(l, (CONV_K, cdim), 1), _layer_spec(l, (1, cdim), 1),
        ],
        out_specs=[pl.BlockSpec((tm, n), lambda i: (i, 0)) for n in widths],
        out_shape=[jax.ShapeDtypeStruct((m, n), dt) for n, dt in zip(widths, dtypes)],
        scratch_shapes=[pltpu.VMEM((tm + CONV_PAD, cdim), F32), pltpu.VMEM((tm + CONV_PAD, cdim), F32)],
        compiler_params=_cp(("arbitrary",)),
        name="in_proj",
    )(h, norm_w, mod5, mod5, w_all, ssd_cw, ssd_cb, ml_cw, ml_cb)


def _ssd_kernel(p_ref, sm_ref, dtb_ref, alog_ref, dsk_ref, nw_ref, e8_ref, o_ref, st_ref):
    @pl.when(pl.program_id(0) == 0)
    def _():
        st_ref[...] = jnp.zeros(st_ref.shape, F32)

    nb = p_ref.shape[0]
    rng = range(nb)
    tri = _tri_mask()
    tri_bf = jnp.where(tri, 1.0, 0.0).astype(BF16)
    e8 = e8_ref[...]
    gw = SSD_GROUPS * SSD_STATE
    nh = SSD_HEADS // SSD_GROUPS
    hw = nh * SSD_HEAD_DIM
    lane = lax.broadcasted_iota(jnp.int32, (CHUNK, hw), 1)
    head_masks = [(lane >= hh * SSD_HEAD_DIM) & (lane < (hh + 1) * SSD_HEAD_DIM) for hh in range(nh)]

    neg_a = -jnp.exp(alog_ref[...])
    dt = [_softplus(sm_ref[b] + dtb_ref[...]) for b in rng]
    a_cum = _cumsum_rows_batched(tri_bf, [dt[b] * neg_a for b in rng])
    a_cum_t = [a_cum[b].T for b in rng]
    ex = _dot01_r2(jnp.concatenate([a_cum[b] for b in rng] + [dt[b] for b in rng], axis=0), e8)
    ac_e = [ex[b * CHUNK:(b + 1) * CHUNK] for b in rng]
    dt_e = [ex[(nb + b) * CHUNK:(nb + b + 1) * CHUNK] for b in rng]
    xs = [p_ref[b, :, D_BRANCH:2 * D_BRANCH].astype(F32) for b in rng]
    xd = [xs[b] * dt_e[b] for b in rng]
    xd_bf = [xd[b].astype(BF16) for b in rng]
    a_last_e = [ac_e[b][CHUNK - 1:CHUNK, :] for b in rng]
    xw_bf = [(xd[b] * jnp.exp(a_last_e[b] - ac_e[b])).astype(BF16) for b in rng]
    ea_e = [jnp.exp(ac_e[b]) for b in rng]

    ys = [[] for _ in rng]
    for g in range(SSD_GROUPS):
        gs = slice(g * hw, (g + 1) * hw)
        bg = [p_ref[b, :, 2 * D_BRANCH + g * SSD_STATE:2 * D_BRANCH + (g + 1) * SSD_STATE] for b in rng]
        cg = [p_ref[b, :, 2 * D_BRANCH + gw + g * SSD_STATE:2 * D_BRANCH + gw + (g + 1) * SSD_STATE]
              for b in rng]
        cb = [_dot_nt(cg[b], bg[b]) for b in rng]
        y_off = [_dot(cg[b], st_ref[b, g].astype(BF16)) for b in rng]
        m_cat, x_cat = [], []
        for b in rng:
            m_parts, x_parts = [], []
            for hh in range(nh):
                h = g * nh + hh
                seg = a_cum[b][:, h:h + 1] - a_cum_t[b][h:h + 1, :]
                dec = jnp.exp(jnp.where(tri, seg, NEG_BIG))
                m_parts.append((cb[b] * dec).astype(BF16))
                x_parts.append(jnp.where(head_masks[hh], xd_bf[b][:, gs], jnp.zeros((CHUNK, hw), BF16)))
            m_cat.append(jnp.concatenate(m_parts, axis=1))
            x_cat.append(jnp.concatenate(x_parts, axis=0))
        y_dg = [_dot(m_cat[b], x_cat[b]) for b in rng]
        s_loc = [_dot_tn(bg[b], xw_bf[b][:, gs]) for b in rng]
        for b in rng:
            ys[b].append(y_off[b] * ea_e[b][:, gs] + y_dg[b])
            st_ref[b, g] = st_ref[b, g] * jnp.exp(a_last_e[b][:, gs]) + s_loc[b]

    for b in rng:
        gate = p_ref[b, :, 0:D_BRANCH].astype(F32)
        y = jnp.concatenate(ys[b], axis=1) + dsk_ref[...] * xs[b]
        o_ref[b] = (_rms(y * gate) * nw_ref[...]).astype(o_ref.dtype)


def _ssd_call(l, p_ssd, p_sm, dt_bias, a_log, d_skip, norm_w, bsz, s):
    nc = s // CHUNK
    m = bsz * s
    e8 = np.zeros((LANES, D_BRANCH), np.float32)
    for h in range(SSD_HEADS):
        e8[h, h * SSD_HEAD_DIM:(h + 1) * SSD_HEAD_DIM] = 1.0
    row = lambda n: pl.BlockSpec((bsz, CHUNK, n), lambda c: (0, c, 0))
    return pl.pallas_call(
        _ssd_kernel,
        grid=(nc,),
        in_specs=[row(_W_SSD), row(LANES),
                  _layer_spec(l, (1, LANES), 2), _layer_spec(l, (1, LANES), 2),
                  _layer_spec(l, (1, D_BRANCH), 2), _layer_spec(l, (1, D_BRANCH), 2),
                  pl.BlockSpec((LANES, D_BRANCH), lambda c: (0, 0))],
        out_specs=row(D_BRANCH),
        out_shape=jax.ShapeDtypeStruct((bsz, s, D_BRANCH), BF16),
        scratch_shapes=[pltpu.VMEM((bsz, SSD_GROUPS, SSD_STATE, 256), F32)],
        compiler_params=_cp(("arbitrary",)),
        name="ssd_mixer",
    )(p_ssd.reshape(bsz, s, _W_SSD), p_sm.reshape(bsz, s, _W_SM), dt_bias, a_log, d_skip, norm_w,
      jnp.asarray(e8, BF16)).reshape(m, D_BRANCH)


def _mlstm_kernel(p_ref, sm_ref, ib_ref, fb_ref, nw_ref, jm_ref, o_ref, c_ref, m_ref):
    @pl.when(pl.program_id(0) == 0)
    def _():
        c_ref[...] = jnp.zeros(c_ref.shape, F32)
        m_ref[...] = jnp.zeros(m_ref.shape, F32)

    nb = p_ref.shape[0]
    rng = range(nb)
    d = D_BRANCH
    tri = _tri_mask()
    tri_bf = jnp.where(tri, 1.0, 0.0).astype(BF16)

    i_log = [sm_ref[b, :, LANES:2 * LANES] + ib_ref[...] for b in rng]
    f_log = [-_softplus(-(sm_ref[b, :, 2 * LANES:3 * LANES] + fb_ref[...])) for b in rng]
    f_cum = _cumsum_rows_batched(tri_bf, f_log)
    r = [i_log[b] - f_cum[b] for b in rng]
    r_t = [r[b].T for b in rng]
    a_col, inter, em, s_old, w_shift = [], [], [], [], []
    for b in rng:
        f_last = f_cum[b][CHUNK - 1:CHUNK, :]
        m_prev = m_ref[b]
        m_new = jnp.maximum(f_last + m_prev, jnp.max(f_last + r[b], axis=0, keepdims=True))
        s_old.append(jnp.exp(f_last + m_prev - m_new))
        w_shift.append(f_last - m_new)
        m_ref[b] = m_new
        g_inter = f_cum[b] + m_prev
        m_t = jnp.maximum(g_inter, f_cum[b] + _cummax_rows(r[b]))
        a_col.append(f_cum[b] - m_t)
        inter.append(jnp.exp(g_inter - m_t))
        em.append(jnp.exp(-m_t))

    ones = jnp.ones((CHUNK, ML_HEAD_DIM), BF16)
    ys = [[] for _ in rng]
    for h in range(ML_HEADS):
        c0 = h * ML_HEAD_DIM
        qh = [p_ref[b, :, d + c0:d + c0 + ML_HEAD_DIM] for b in rng]
        kh = [p_ref[b, :, 2 * d + c0:2 * d + c0 + ML_HEAD_DIM] for b in rng]
        v_ext = [jnp.concatenate([p_ref[b, :, 3 * d + c0:3 * d + c0 + ML_HEAD_DIM], ones], axis=1)
                 for b in rng]
        qk = [_dot_nt(qh[b], kh[b]) for b in rng]
        qc = [_dot(qh[b], c_ref[b, h].astype(BF16)) for b in rng]
        sm = []
        for b in rng:
            dmat = jnp.where(tri, a_col[b][:, h:h + 1] + r_t[b][h:h + 1, :], NEG_BIG)
            sm.append((qk[b] * jnp.exp(dmat)).astype(BF16))
        numden = [_dot(sm[b], v_ext[b]) + inter[b][:, h:h + 1] * qc[b] for b in rng]
        kw_t = [(kh[b].T.astype(F32) * jnp.exp(r_t[b][h:h + 1, :] + w_shift[b][:, h:h + 1])).astype(BF16)
                for b in rng]
        upd = [_dot(kw_t[b], v_ext[b]) for b in rng]
        for b in rng:
            num = numden[b][:, 0:ML_HEAD_DIM]
            den = numden[b][:, ML_HEAD_DIM:]
            o_gate = p_ref[b, :, 4 * d + c0:4 * d + c0 + ML_HEAD_DIM].astype(F32)
            ys[b].append(o_gate * (num / jnp.maximum(jnp.abs(den), em[b][:, h:h + 1])))
            c_ref[b, h] = s_old[b][:, h:h + 1] * c_ref[b, h] + upd[b]

    y = [jnp.concatenate(ys[b], axis=1) for b in rng]
    ms = [_dot((y[b] * y[b]).astype(BF16), jm_ref[...]) for b in rng]
    for b in rng:
        gate = p_ref[b, :, 0:d].astype(F32)
        o_ref[b] = (y[b] * lax.rsqrt(ms[b] + NORM_EPS) * nw_ref[...] * gate).astype(o_ref.dtype)


def _mlstm_call(l, p_ml, p_sm, i_bias, f_bias, norm_w, bsz, s):
    nc = s // CHUNK
    m = bsz * s
    row = lambda n: pl.BlockSpec((bsz, CHUNK, n), lambda c: (0, c, 0))
    jm = np.kron(np.eye(ML_HEADS), np.full((ML_HEAD_DIM, ML_HEAD_DIM), 1.0 / ML_HEAD_DIM))
    return pl.pallas_call(
        _mlstm_kernel,
        grid=(nc,),
        in_specs=[row(_W_ML), row(_W_SM),
                  _layer_spec(l, (1, LANES), 2), _layer_spec(l, (1, LANES), 2),
                  _layer_spec(l, (1, D_BRANCH), 2),
                  pl.BlockSpec((D_BRANCH, D_BRANCH), lambda c: (0, 0))],
        out_specs=row(D_BRANCH),
        out_shape=jax.ShapeDtypeStruct((bsz, s, D_BRANCH), BF16),
        scratch_shapes=[pltpu.VMEM((bsz, ML_HEADS, ML_HEAD_DIM, 2 * ML_HEAD_DIM), F32),
                        pltpu.VMEM((bsz, 1, LANES), F32)],
        compiler_params=_cp(("arbitrary",)),
        name="mlstm_mixer",
    )(p_ml.reshape(bsz, s, _W_ML), p_sm.reshape(bsz, s, _W_SM), i_bias, f_bias,
      norm_w, jnp.asarray(jm, BF16)).reshape(m, D_BRANCH)


def _ret_log_gamma():
    return [math.log1p(-2.0 ** (-(RET_DECAY_BASE + h))) for h in range(RET_HEADS)]


def _ret_consts():
    lg = _ret_log_gamma()
    pos = np.arange(CHUNK, dtype=np.float64)
    rel = pos[:, None] - pos[None, :]
    dm = np.stack([np.where(rel >= 0, np.exp(np.maximum(rel, 0.0) * g), 0.0) for g in lg])
    from_start = np.concatenate(
        [np.repeat(np.exp((pos + 1.0) * g)[:, None], RET_V, axis=1) for g in lg], axis=1)
    to_end = np.concatenate(
        [np.repeat(np.exp((CHUNK - 1.0 - pos) * g)[:, None], RET_QK, axis=1) for g in lg], axis=1)
    qw, vw = RET_HEADS * RET_QK, RET_HEADS * RET_V
    chunk_decay = np.zeros((qw, vw))
    for h, g in enumerate(lg):
        chunk_decay[h * RET_QK:(h + 1) * RET_QK, h * RET_V:(h + 1) * RET_V] = math.exp(CHUNK * g)
    block_diag = (chunk_decay > 0).astype(np.float64)
    swap = np.zeros((qw, qw))
    for j in range(qw):
        base, off = (j // RET_QK) * RET_QK, j % RET_QK
        swap[base + (off + RET_QK // 2) % RET_QK, j] = 1.0
    f = lambda a: jnp.asarray(a, F32)
    return f(dm), f(from_start), f(to_end), f(chunk_decay), f(block_diag), jnp.asarray(swap, BF16)


def _ret_kernel(p_ref, cos_ref, sin_ref, dm_ref, fs_ref, te_ref, cd_ref, bd_ref, sw_ref, nw_ref,
                o_ref, r_ref):
    @pl.when(pl.program_id(0) == 0)
    def _():
        r_ref[...] = jnp.zeros(r_ref.shape, F32)

    nb = p_ref.shape[0]
    d = D_BRANCH
    qw = RET_HEADS * RET_QK
    sw = sw_ref[...]
    lane = lax.broadcasted_iota(jnp.int32, (CHUNK, qw), 1)
    head_masks = [(lane >= h * RET_QK) & (lane < (h + 1) * RET_QK) for h in range(RET_HEADS)]

    def v_of(b):
        return p_ref[b, :, d + 2 * qw:2 * d + 2 * qw]

    q_raw = [p_ref[b, :, d:d + qw] for b in range(nb)]
    k_raw = [p_ref[b, :, d + qw:d + 2 * qw] for b in range(nb)]
    q_sw = [_dot(q_raw[b], sw) for b in range(nb)]
    k_sw = [_dot(k_raw[b], sw) for b in range(nb)]
    qs, ks = [], []
    for b in range(nb):
        cos_e = jnp.concatenate([cos_ref[b]] * (qw // LANES), axis=1)
        sin_e = jnp.concatenate([sin_ref[b]] * (qw // LANES), axis=1)
        qs.append(q_raw[b].astype(F32) * cos_e + q_sw[b] * sin_e)
        ks.append((k_raw[b].astype(F32) * cos_e + k_sw[b] * sin_e) * (RET_QK ** -0.5))
    k_bf = [k.astype(BF16) for k in ks]
    ys = [_dot(qs[b].astype(BF16), r_ref[b].astype(BF16)) * fs_ref[...] for b in range(nb)]
    for h in range(RET_HEADS):
        scores = [(_dot_nt(jnp.where(head_masks[h], qs[b], jnp.zeros_like(qs[b])).astype(BF16), k_bf[b])
                   * dm_ref[h]).astype(BF16) for b in range(nb)]
        inner = [_dot(scores[b], v_of(b)[:, h * RET_V:(h + 1) * RET_V]) for b in range(nb)]
        pad_l, pad_r = h * RET_V, (RET_HEADS - 1 - h) * RET_V
        for b in range(nb):
            parts = ([jnp.zeros((CHUNK, pad_l), F32)] if pad_l else []) + [inner[b]] + \
                    ([jnp.zeros((CHUNK, pad_r), F32)] if pad_r else [])
            ys[b] = ys[b] + jnp.concatenate(parts, axis=1)
    upd = [_dot_tn((ks[b] * te_ref[...]).astype(BF16), v_of(b)) for b in range(nb)]
    for b in range(nb):
        r_ref[b] = r_ref[b] * cd_ref[...] + upd[b] * bd_ref[...]
    for b in range(nb):
        gate = p_ref[b, :, 0:d].astype(F32)
        outs = [_rms(ys[b][:, h * RET_V:(h + 1) * RET_V]) for h in range(RET_HEADS)]
        o_ref[b] = (jnp.concatenate(outs, axis=1) * nw_ref[...] * gate).astype(o_ref.dtype)


def _ret_call(l, p_ret, cos_t, sin_t, consts, norm_w, bsz, s):
    nc = s // CHUNK
    m = bsz * s
    qw, vw = RET_HEADS * RET_QK, RET_HEADS * RET_V
    dm, fs, te, cd, bd, sw = consts
    const = lambda shape: pl.BlockSpec(shape, lambda c: (0,) * len(shape))
    row = lambda n: pl.BlockSpec((bsz, CHUNK, n), lambda c: (0, c, 0))
    return pl.pallas_call(
        _ret_kernel,
        grid=(nc,),
        in_specs=[row(_W_RET), row(LANES), row(LANES),
                  const((RET_HEADS, CHUNK, CHUNK)), const((CHUNK, vw)), const((CHUNK, qw)),
                  const((qw, vw)), const((qw, vw)), const((qw, qw)),
                  _layer_spec(l, (1, D_BRANCH), 2)],
        out_specs=row(D_BRANCH),
        out_shape=jax.ShapeDtypeStruct((bsz, s, D_BRANCH), BF16),
        scratch_shapes=[pltpu.VMEM((bsz, qw, vw), F32)],
        compiler_params=_cp(("arbitrary",)),
        name="retention_mixer",
    )(p_ret.reshape(bsz, s, _W_RET), cos_t.reshape(bsz, s, LANES), sin_t.reshape(bsz, s, LANES),
      dm, fs, te, cd, bd, sw, norm_w).reshape(m, D_BRANCH)


def _s5_compact(lam_re, lam_im, b_re, b_im, c_re, c_im, log_step):
    depth = lam_re.shape[0]
    nb, ngl, sub = S5_LANE_BLOCKS, S5_LOCAL_GROUPS, S5_SUB
    step = jnp.exp(log_step.astype(F32))[..., None]
    lr = jnp.minimum(lam_re.astype(F32), -1e-4)
    li = lam_im.astype(F32)
    mag = jnp.exp(lr * step)
    ang = li * step
    ab_re = mag * jnp.cos(ang)
    ab_im = mag * jnp.sin(ang)
    den = lr * lr + li * li
    coef_re = ((ab_re - 1.0) * lr + ab_im * li) / den
    coef_im = (ab_im * lr - (ab_re - 1.0) * li) / den
    br_t, bi_t = jnp.swapaxes(b_re.astype(F32), -1, -2), jnp.swapaxes(b_im.astype(F32), -1, -2)
    cf_re, cf_im = coef_re[:, :, None, :], coef_im[:, :, None, :]
    bb_re = cf_re * br_t - cf_im * bi_t
    bb_im = cf_re * bi_t + cf_im * br_t
    pw_re, pw_im = [jnp.ones_like(ab_re)], [jnp.zeros_like(ab_im)]
    for _ in range(sub):
        pr, pi = pw_re[-1], pw_im[-1]
        pw_re.append(pr * ab_re - pi * ab_im)
        pw_im.append(pr * ab_im + pi * ab_re)
    cr, ci = c_re.astype(F32), c_im.astype(F32)
    cr_t, ci_t = jnp.swapaxes(cr, -1, -2), jnp.swapaxes(ci, -1, -2)
    hp = lax.Precision.HIGHEST
    abb_re = [pw_re[k][:, :, None, :] * bb_re - pw_im[k][:, :, None, :] * bb_im for k in range(sub)]
    abb_im = [pw_re[k][:, :, None, :] * bb_im + pw_im[k][:, :, None, :] * bb_re for k in range(sub)]
    taps_t = [jnp.einsum('lgip,lgop->lgio', abb_re[k], cr, precision=hp)
              - jnp.einsum('lgip,lgop->lgio', abb_im[k], ci, precision=hp) for k in range(sub)]

    def by_block(rows):
        a = jnp.stack(rows).reshape(len(rows), depth, nb, ngl, rows[0].shape[-2], rows[0].shape[-1])
        return a.transpose(1, 2, 0, 3, 4, 5).reshape(depth, nb, -1, rows[0].shape[-1])

    c_in = jnp.concatenate([by_block([abb_re[sub - 1 - t] for t in range(sub)]),
                            by_block([abb_im[sub - 1 - t] for t in range(sub)])], axis=-1)
    zero = jnp.zeros_like(taps_t[0])
    toe = by_block([jnp.concatenate([zero] * t + taps_t[:sub - t], axis=-1) for t in range(sub)])

    def rows_x(c_t, s_t, sign):
        cols = [c_t * pw_re[k][..., None] - sign * s_t * pw_im[k][..., None] for k in range(1, sub + 1)]
        return jnp.stack(cols, axis=-2).reshape(depth, nb, ngl * S5_STATE, sub * S5_GROUP)

    x_re = rows_x(cr_t, ci_t, 1.0)
    x_im = -(jnp.stack([cr_t * pw_im[k][..., None] + ci_t * pw_re[k][..., None]
                        for k in range(1, sub + 1)], axis=-2)
             .reshape(depth, nb, ngl * S5_STATE, sub * S5_GROUP))
    c_out = jnp.concatenate([toe, x_re, x_im], axis=2)
    a8 = jnp.concatenate([pw_re[sub].reshape(depth, nb, 1, ngl * S5_STATE),
                          pw_im[sub].reshape(depth, nb, 1, ngl * S5_STATE)], axis=-1)
    return c_in.astype(BF16), c_out.astype(BF16), a8


def _s5_expanders():
    kin = S5_SUB * LANES
    half = S5_LOCAL_GROUPS * S5_STATE
    sel_in = np.zeros((LANES, 2 * half), np.float32)
    csel_in = np.zeros((LANES, 2 * half), np.float32)
    for col in range(2 * half):
        sel_in[(col // half) * S5_STATE + col % S5_STATE, col] = 1.0
        csel_in[(col % half) // S5_STATE, col] = 1.0
    sel_out = np.zeros((LANES, kin), np.float32)
    csel_out = np.zeros((LANES, kin), np.float32)
    for col in range(kin):
        sel_out[(col // LANES) * S5_GROUP + col % S5_GROUP, col] = 1.0
        csel_out[(col // S5_GROUP) % S5_LOCAL_GROUPS, col] = 1.0
    rsel_in = np.zeros((kin, LANES), np.float32)
    for r in range(kin):
        rsel_in[r, (r // S5_GROUP) % S5_LOCAL_GROUPS] = 1.0
    rsel_out = np.zeros((2 * kin, LANES), np.float32)
    rsel_out[:kin] = rsel_in
    for r in range(kin):
        rsel_out[kin + r, (r % half) // S5_STATE] = 1.0
    return tuple(jnp.asarray(a, BF16) for a in (sel_in, csel_in, rsel_in, sel_out, csel_out, rsel_out))


def _s5_fold(a, bsz, s):
    j = s // S5_SUB
    a = a.reshape(bsz, j, S5_SUB, S5_LANE_BLOCKS, LANES).transpose(1, 0, 3, 2, 4)
    return a.reshape(j * bsz, S5_LANE_BLOCKS * S5_SUB * LANES)


def _s5_unfold(a, bsz, s):
    j = s // S5_SUB
    a = a.reshape(j, bsz, S5_LANE_BLOCKS, S5_SUB, LANES).transpose(1, 0, 3, 2, 4)
    return a.reshape(bsz * s, D_BRANCH)


S5_ROW_BLOCK = 512


def _s5_scan_kernel(u_ref, cin_ref, cout_ref, a8_ref, sin_ref, csin_ref, rsin_ref,
                    sout_ref, csout_ref, rsout_ref, o_ref, x_ref, min_ref, mout_ref, *, bsz):
    rows = u_ref.shape[0]
    half = S5_LOCAL_GROUPS * S5_STATE
    rb = min(S5_ROW_BLOCK, rows)

    min_ref[...] = (_dot(cin_ref[...], sin_ref[...]) * _dot(rsin_ref[...], csin_ref[...])).astype(BF16)
    mout_ref[...] = (_dot(cout_ref[...], sout_ref[...]) * _dot(rsout_ref[...], csout_ref[...])).astype(BF16)

    for r0 in range(0, rows, rb):
        x_ref[r0:r0 + rb, :] = _dot(u_ref[r0:r0 + rb, :], min_ref[...])

    a_re = a8_ref[:, 0:half]
    a_im = a8_ref[:, half:]

    def body(i, carry):
        x_re, x_im = carry
        r0 = pl.multiple_of(i * bsz, bsz)
        loc = x_ref[pl.ds(r0, bsz), :]
        x_ref[pl.ds(r0, bsz), :] = jnp.concatenate([x_re, x_im], axis=1)
        n_re = a_re * x_re - a_im * x_im + loc[:, 0:half]
        n_im = a_re * x_im + a_im * x_re + loc[:, half:]
        return n_re, n_im

    zero = jnp.zeros((bsz, half), F32)
    lax.fori_loop(0, rows // bsz, body, (zero, zero), unroll=8)

    kin = u_ref.shape[1]
    nt_w = 2 * LANES
    for r0 in range(0, rows, rb):
        xb = x_ref[r0:r0 + rb, :].astype(BF16)
        for c0 in range(0, kin, nt_w):
            k_u = c0 + nt_w
            y = _dot(u_ref[r0:r0 + rb, 0:k_u], mout_ref[0:k_u, c0:c0 + nt_w])
            y = y + _dot(xb, mout_ref[kin:2 * kin, c0:c0 + nt_w])
            o_ref[r0:r0 + rb, c0:c0 + nt_w] = y.astype(o_ref.dtype)


def _s5_scan_call(l, u_fold, c_in, c_out, a8, expanders, bsz):
    rows = u_fold.shape[0]
    kin = S5_SUB * LANES
    lspec = lambda shape: pl.BlockSpec((None, None) + shape, lambda lb: (l, lb, 0, 0))
    const = lambda a: pl.BlockSpec(a.shape, lambda lb: (0, 0))
    sel_in, csel_in, rsel_in, sel_out, csel_out, rsel_out = expanders
    return pl.pallas_call(
        functools.partial(_s5_scan_kernel, bsz=bsz),
        grid=(S5_LANE_BLOCKS,),
        in_specs=[pl.BlockSpec((rows, kin), lambda lb: (0, lb)),
                  lspec((kin, LANES)), lspec((2 * kin, LANES)), lspec((1, kin)),
                  const(sel_in), const(csel_in), const(rsel_in),
                  const(sel_out), const(csel_out), const(rsel_out)],
        out_specs=pl.BlockSpec((rows, kin), lambda lb: (0, lb)),
        out_shape=jax.ShapeDtypeStruct(u_fold.shape, BF16),
        scratch_shapes=[pltpu.VMEM((rows, kin), F32),
                        pltpu.VMEM((kin, kin), BF16), pltpu.VMEM((2 * kin, kin), BF16)],
        compiler_params=_cp(("parallel",)),
        name="s5_scan",
    )(u_fold, c_in, c_out, a8, sel_in, csel_in, rsel_in, sel_out, csel_out, rsel_out)


def _outproj_kernel(y_ssd, y_ml, s5g_ref, s5u_ref, s5y_ref, y_ret, w_ref, h_ref, g_ref, fw_ref,
                    d_ref, wg_ref, bg_ref, nw5_ref, o_ref, *, final_norm):
    d = D_BRANCH
    u = s5u_ref[...].astype(F32)
    y = s5y_ref[...].astype(F32) + d_ref[...] * u
    y = 0.5 * y * (1.0 + jnp.tanh(math.sqrt(2.0 / math.pi) * (y + 0.044715 * (y * y * y))))
    g = _dot(y.astype(BF16), wg_ref[...]) + bg_ref[...]
    y = g[:, 0:d] * _sigmoid(g[:, d:2 * d])
    y_s5 = (_rms(y) * nw5_ref[...] * s5g_ref[...].astype(F32)).astype(BF16)

    acc = _dot(y_ssd[...], w_ref[0 * d:1 * d, :])
    acc = acc + _dot(y_ml[...], w_ref[1 * d:2 * d, :])
    acc = acc + _dot(y_s5, w_ref[2 * d:3 * d, :])
    acc = acc + _dot(y_ret[...], w_ref[3 * d:4 * d, :])
    h = h_ref[...] + g_ref[...] * acc
    if final_norm:
        h = _rms(h) * fw_ref[...]
    o_ref[...] = h


def _outproj_call(l, y_ssd, y_ml, p_s5g, p_s5u, y_ssm, y_ret, w_out, h, mod5, final_w,
                  s5_d, w_glu, b_glu, s5_nw, s, final_norm):
    m = h.shape[0]
    tm = min(ROW_TILE, s)
    per_batch = s // tm
    yspec = pl.BlockSpec((tm, D_BRANCH), lambda i: (i, 0))
    return pl.pallas_call(
        functools.partial(_outproj_kernel, final_norm=final_norm),
        grid=(m // tm,),
        in_specs=[yspec] * 6 + [
            _layer_spec(l, (4 * D_BRANCH, D_MODEL), 1),
            pl.BlockSpec((tm, D_MODEL), lambda i: (i, 0)),
            pl.BlockSpec((None, None, None, 1, D_MODEL), lambda i: (l, i // per_batch, 2, 0, 0)),
            pl.BlockSpec((1, D_MODEL), lambda i: (0, 0)),
            _layer_spec(l, (1, D_BRANCH), 1), _layer_spec(l, (D_BRANCH, 2 * D_BRANCH), 1),
            _layer_spec(l, (1, 2 * D_BRANCH), 1), _layer_spec(l, (1, D_BRANCH), 1),
        ],
        out_specs=pl.BlockSpec((tm, D_MODEL), lambda i: (i, 0)),
        out_shape=jax.ShapeDtypeStruct((m, D_MODEL), F32),
        compiler_params=_cp(("parallel",)),
        name="out_proj",
    )(y_ssd, y_ml, p_s5g, p_s5u, y_ssm, y_ret, w_out, h, mod5, final_w, s5_d, w_glu, b_glu, s5_nw)


_PACK_PIECES = ((0, _O_SSD_Z, _W_SSD), (_W_SSD, _O_ML_Z, _W_ML), (_W_SSD + _W_ML, _O_S5_Z, _W_S5),
                (_W_SSD + _W_ML + _W_S5, _O_RET_Z, _W_RET),
                (_W_ALL - _W_SM, _O_SSD_DT, _O_ML_Z - _O_SSD_DT),
                (_W_ALL - _W_SM + LANES, _O_ML_I, _O_ML_F - _O_ML_I),
                (_W_ALL - _W_SM + 2 * LANES, _O_ML_F, _O_S5_Z - _O_ML_F))


def _pack_kernel(w_ref, o_ref):
    o_ref[:, _W_ALL - _W_SM:_W_ALL] = jnp.zeros((o_ref.shape[0], _W_SM), BF16)
    for dst, src, width in _PACK_PIECES:
        o_ref[:, dst:dst + width] = w_ref[:, src:src + width].astype(BF16)


def _pack_w_in(w_in):
    depth, d, n = w_in.shape
    assert n == _D_IN
    tr = 256
    return pl.pallas_call(
        _pack_kernel,
        grid=(depth, d // tr),
        in_specs=[pl.BlockSpec((None, tr, n), lambda l, i: (l, i, 0))],
        out_specs=pl.BlockSpec((None, tr, _W_ALL), lambda l, i: (l, i, 0)),
        out_shape=jax.ShapeDtypeStruct((depth, d, _W_ALL), BF16),
        compiler_params=_cp(("parallel", "parallel")),
        name="pack_w_in",
    )(w_in)


def _pad_lanes(v):
    return jnp.pad(v.astype(F32), ((0, 0), (0, LANES - v.shape[-1])))[:, None, :]


def kernel(x, c, positions, norm_w, w_ada, b_ada, w_in, w_out, ssd_conv_w, ssd_conv_b, ssd_dt_bias, ssd_a_log, ssd_d, ssd_norm_w, ml_conv_w, ml_conv_b, ml_i_bias, ml_f_bias, ml_norm_w, s5_lambda_re, s5_lambda_im, s5_b_re, s5_b_im, s5_c_re, s5_c_im, s5_d, s5_log_step, s5_w_glu, s5_b_glu, s5_norm_w, ret_norm_w, final_norm_w):
    bsz, s, d = x.shape
    depth = w_in.shape[0]
    assert d == D_MODEL and s % CHUNK == 0 and bsz % 8 == 0
    m = bsz * s
    row3 = lambda a: a.astype(F32)[:, None, :]

    mod = _mod_call(c.astype(F32), w_ada, b_ada)
    mod5 = mod.reshape(depth, bsz, 3, 1, D_MODEL)
    cos_t, sin_t = _rope_call(positions)
    ret_consts = _ret_consts()
    w_all = _pack_w_in(w_in.astype(BF16))
    w_out_bf = w_out.astype(BF16)
    w_glu_bf = s5_w_glu.astype(BF16)
    c_in, c_out, a8 = _s5_compact(s5_lambda_re, s5_lambda_im, s5_b_re, s5_b_im, s5_c_re, s5_c_im,
                                  s5_log_step)
    s5_exp = _s5_expanders()
    norm_w3, final_w = row3(norm_w), final_norm_w.reshape(1, D_MODEL).astype(F32)
    ssd_cb3, ml_cb3 = row3(ssd_conv_b), row3(ml_conv_b)
    dt_bias3, a_log3 = _pad_lanes(ssd_dt_bias), _pad_lanes(ssd_a_log)
    d_skip3 = row3(jnp.repeat(ssd_d, SSD_HEAD_DIM, axis=-1))
    i_bias3, f_bias3 = _pad_lanes(ml_i_bias), _pad_lanes(ml_f_bias)
    ssd_nw3, ml_nw3, s5_nw3, ret_nw3 = row3(ssd_norm_w), row3(ml_norm_w), row3(s5_norm_w), row3(ret_norm_w)
    s5_d3, b_glu3 = row3(s5_d), row3(s5_b_glu)

    h = x.reshape(m, D_MODEL)
    for l in range(depth):
        p_ssd, p_ml, p_s5g, p_s5u, p_ret, p_sm = _inproj_call(
            l, h, norm_w3, mod5, w_all, ssd_conv_w, ssd_cb3, ml_conv_w, ml_cb3, s)
        y_ssd = _ssd_call(l, p_ssd, p_sm, dt_bias3, a_log3, d_skip3, ssd_nw3, bsz, s)
        y_ml = _mlstm_call(l, p_ml, p_sm, i_bias3, f_bias3, ml_nw3, bsz, s)
        y_fold = _s5_scan_call(l, _s5_fold(p_s5u, bsz, s), c_in, c_out, a8, s5_exp, bsz)
        y_ssm = _s5_unfold(y_fold, bsz, s)
        y_ret = _ret_call(l, p_ret, cos_t, sin_t, ret_consts, ret_nw3, bsz, s)
        h = _outproj_call(l, y_ssd, y_ml, p_s5g, p_s5u, y_ssm, y_ret, w_out_bf, h, mod5, final_w,
                          s5_d3, w_glu_bf, b_glu3, s5_nw3, s, final_norm=(l == depth - 1))
    return h.reshape(bsz, s, D_MODEL)
```

```python
import functools
import math

import numpy as np
import jax
import jax.numpy as jnp
from jax import lax
from jax.experimental import pallas as pl
from jax.experimental.pallas import tpu as pltpu

F32 = jnp.float32
BF16 = jnp.bfloat16

D_MODEL = 1024
D_BRANCH = 512
CHUNK = 128
CONV_K = 4
CONV_PAD = 8
NORM_EPS = 1e-6
NEG_BIG = -1e30

SSD_HEADS = 8
SSD_HEAD_DIM = 64
SSD_GROUPS = 2
SSD_STATE = 128
ML_HEADS = 4
ML_HEAD_DIM = 128
S5_GROUPS = 32
S5_GROUP = 16
S5_STATE = 64
S5_SUB = 8
S5_LANE_BLOCKS = 4
S5_LOCAL_GROUPS = S5_GROUPS // S5_LANE_BLOCKS
RET_HEADS = 4
RET_QK = 64
RET_V = 128
RET_DECAY_BASE = 5.0
ROPE_BASE = 10000.0
LANES = 128

_O_SSD_Z, _O_SSD_DT = 0, 1536
_O_ML_Z, _O_ML_I, _O_ML_F = 1544, 4104, 4108
_O_S5_Z = 4112
_O_RET_Z = 5136
_D_IN = 6672
_W_SSD, _W_ML, _W_S5, _W_RET, _W_SM = 1536, 2560, 1024, 1536, 384
_W_ALL = _W_SSD + _W_ML + _W_S5 + _W_RET + _W_SM
_W_S5G = _W_S5 - D_BRANCH

VMEM_LIMIT = 56 * 1024 * 1024
ROW_TILE = 512


def _cp(sem):
    return pltpu.CompilerParams(dimension_semantics=sem, vmem_limit_bytes=VMEM_LIMIT)


def _layer_spec(l, shape, nargs):
    return pl.BlockSpec((None,) + tuple(shape), lambda *_: (l,) + (0,) * len(shape))


def _dot(a, b):
    return jnp.dot(a, b, preferred_element_type=F32)


def _dot_nt(a, b):
    return lax.dot_general(a, b, (((1,), (1,)), ((), ())), preferred_element_type=F32)


def _dot_tn(a, b):
    return lax.dot_general(a, b, (((0,), (0,)), ((), ())), preferred_element_type=F32)


def _sigmoid(x):
    return 0.5 + 0.5 * jnp.tanh(0.5 * x)


def _silu(x):
    hx = 0.5 * x
    return hx + hx * jnp.tanh(hx)


def _softplus(x):
    return jnp.maximum(x, 0.0) + jnp.log(1.0 + jnp.exp(-jnp.abs(x)))


def _split3(v):
    hi = v.astype(BF16)
    r1 = v - hi.astype(F32)
    mid = r1.astype(BF16)
    lo = (r1 - mid.astype(F32)).astype(BF16)
    return hi, mid, lo


def _dot01_r2(v, m01):
    hi = v.astype(BF16)
    lo = (v - hi.astype(F32)).astype(BF16)
    return _dot(hi, m01) + _dot(lo, m01)


def _cumsum_rows_batched(tri_bf, xs):
    terms = []
    for x in xs:
        terms.extend(_split3(x))
    res = _dot(tri_bf, jnp.concatenate(terms, axis=1))
    w = xs[0].shape[1]
    return [res[:, (3 * i) * w:(3 * i + 1) * w] + res[:, (3 * i + 1) * w:(3 * i + 2) * w]
            + res[:, (3 * i + 2) * w:(3 * i + 3) * w] for i in range(len(xs))]


def _cummax_rows(x):
    n = x.shape[0]
    row = lax.broadcasted_iota(jnp.int32, x.shape, 0)
    sh = 1
    while sh < n:
        x = jnp.maximum(x, jnp.where(row >= sh, pltpu.roll(x, sh, axis=0), NEG_BIG))
        sh *= 2
    return x


def _tri_mask():
    row = lax.broadcasted_iota(jnp.int32, (CHUNK, CHUNK), 0)
    col = lax.broadcasted_iota(jnp.int32, (CHUNK, CHUNK), 1)
    return row >= col


def _rms(y):
    return y * lax.rsqrt(jnp.mean(y * y, axis=-1, keepdims=True) + NORM_EPS)


def _mod_kernel(c_ref, w_ref, b_ref, o_ref):
    cond = _silu(c_ref[...])
    o_ref[0] = _dot(cond.astype(BF16), w_ref[0].astype(BF16)) + b_ref[0]


def _mod_call(c, w_ada, b_ada):
    depth, d, n = w_ada.shape
    bsz = c.shape[0]
    tn = 1024
    return pl.pallas_call(
        _mod_kernel,
        grid=(depth, n // tn),
        in_specs=[
            pl.BlockSpec((bsz, d), lambda l, j: (0, 0)),
            pl.BlockSpec((1, d, tn), lambda l, j: (l, 0, j)),
            pl.BlockSpec((1, 1, tn), lambda l, j: (l, 0, j)),
        ],
        out_specs=pl.BlockSpec((1, bsz, tn), lambda l, j: (l, 0, j)),
        out_shape=jax.ShapeDtypeStruct((depth, bsz, n), F32),
        compiler_params=_cp(("parallel", "parallel")),
        name="adaln_mod",
    )(c, w_ada, b_ada.reshape(depth, 1, n))


def _rope_kernel(pos_ref, f_ref, sg_ref, cos_ref, sin_ref):
    ang = pos_ref[...] * f_ref[...]
    cos_ref[...] = jnp.cos(ang)
    sin_ref[...] = jnp.sin(ang) * sg_ref[...]


def _rope_call(positions):
    bsz, s = positions.shape
    m = bsz * s
    half = RET_QK // 2
    inv_freq = jnp.exp(-math.log(ROPE_BASE) * jnp.arange(half, dtype=F32) / half)
    f128 = jnp.tile(inv_freq, LANES // half).reshape(1, LANES)
    sign = np.tile(np.concatenate([-np.ones(half), np.ones(half)]), LANES // RET_QK)
    sign = jnp.asarray(sign.reshape(1, LANES), F32)
    pos = jnp.broadcast_to(positions.astype(F32).reshape(m, 1), (m, LANES))
    tm = min(1024, m)
    return pl.pallas_call(
        _rope_kernel,
        grid=(m // tm,),
        in_specs=[
            pl.BlockSpec((tm, LANES), lambda i: (i, 0)),
            pl.BlockSpec((1, LANES), lambda i: (0, 0)),
            pl.BlockSpec((1, LANES), lambda i: (0, 0)),
        ],
        out_specs=[pl.BlockSpec((tm, LANES), lambda i: (i, 0))] * 2,
        out_shape=[jax.ShapeDtypeStruct((m, LANES), F32)] * 2,
        compiler_params=_cp(("parallel",)),
        name="rope_tables",
    )(pos, f128, sign)


def _conv_silu(x, buf_ref, cw_ref, cb_ref):
    n = x.shape[0]
    buf_ref[CONV_PAD:CONV_PAD + n, :] = x
    acc = cb_ref[...] + cw_ref[CONV_K - 1:CONV_K, :] * x
    for k in range(CONV_K - 1):
        off = CONV_PAD - (CONV_K - 1) + k
        acc = acc + cw_ref[k:k + 1, :] * buf_ref[off:off + n, :]
    buf_ref[0:CONV_PAD, :] = x[n - CONV_PAD:n, :]
    return _silu(acc)


def _inproj_kernel(h_ref, nw_ref, sc_ref, sh_ref, w_ref, scw_ref, scb_ref, mcw_ref, mcb_ref,
                   o_ssd, o_ml, o_s5g, o_s5u, o_ret, o_sm, sbuf_ref, mbuf_ref, *, per_batch):
    @pl.when(pl.program_id(0) % per_batch == 0)
    def _():
        sbuf_ref[0:CONV_PAD, :] = jnp.zeros((CONV_PAD, sbuf_ref.shape[1]), F32)
        mbuf_ref[0:CONV_PAD, :] = jnp.zeros((CONV_PAD, mbuf_ref.shape[1]), F32)

    x = h_ref[...]
    hn = _rms(x) * nw_ref[...]
    hb = (hn * (1.0 + sc_ref[...]) + sh_ref[...]).astype(BF16)
    d = D_BRANCH

    def proj(col, width):
        return _dot(hb, w_ref[:, col:col + width])

    col = 0
    o_ssd[:, 0:d] = _silu(proj(col, d)).astype(BF16)
    o_ssd[:, d:3 * d] = _conv_silu(proj(col + d, 2 * d), sbuf_ref, scw_ref, scb_ref).astype(BF16)
    col += _W_SSD
    o_ml[:, 0:d] = _silu(proj(col, d)).astype(BF16)
    qk = _conv_silu(proj(col + d, 2 * d), mbuf_ref, mcw_ref, mcb_ref)
    o_ml[:, d:2 * d] = qk[:, 0:d].astype(BF16)
    o_ml[:, 2 * d:3 * d] = (qk[:, d:2 * d] * (ML_HEAD_DIM ** -0.5)).astype(BF16)
    o_ml[:, 3 * d:4 * d] = proj(col + 3 * d, d).astype(BF16)
    o_ml[:, 4 * d:5 * d] = _sigmoid(proj(col + 4 * d, d)).astype(BF16)
    col += _W_ML
    o_s5g[...] = _silu(proj(col, d)).astype(BF16)
    o_s5u[...] = proj(col + d, d).astype(BF16)
    col += _W_S5
    o_ret[:, 0:d] = _silu(proj(col, d)).astype(BF16)
    o_ret[:, d:3 * d] = proj(col + d, 2 * d).astype(BF16)
    col += _W_RET
    o_sm[...] = proj(col, _W_SM)


def _inproj_call(l, h, norm_w, mod5, w_all, ssd_cw, ssd_cb, ml_cw, ml_cb, s):
    m = h.shape[0]
    tm = min(ROW_TILE, s)
    per_batch = s // tm
    widths = (_W_SSD, _W_ML, _W_S5G, D_BRANCH, _W_RET, _W_SM)
    dtypes = (BF16, BF16, BF16, BF16, BF16, F32)
    cdim = 2 * D_BRANCH
    mod_spec = lambda which: pl.BlockSpec((None, None, None, 1, D_MODEL),
                                          lambda i: (l, i // per_batch, which, 0, 0))
    return pl.pallas_call(
        functools.partial(_inproj_kernel, per_batch=per_batch),
        grid=(m // tm,),
        in_specs=[
            pl.BlockSpec((tm, D_MODEL), lambda i: (i, 0)),
            _layer_spec(l, (1, D_MODEL), 1),
            mod_spec(1), mod_spec(0),
            pl.BlockSpec((None, D_MODEL, _W_ALL), lambda i: (l, 0, 0), pipeline_mode=pl.Buffered(1)),
            _layer_spec(l, (CONV_K, cdim), 1), _layer_spec(l, (1, cdim), 1),
            _layer_spec(l, (CONV_K, cdim), 1), _layer_spec(l, (1, cdim), 1),
        ],
        out_specs=[pl.BlockSpec((tm, n), lambda i: (i, 0)) for n in widths],
        out_shape=[jax.ShapeDtypeStruct((m, n), dt) for n, dt in zip(widths, dtypes)],
        scratch_shapes=[pltpu.VMEM((tm + CONV_PAD, cdim), F32), pltpu.VMEM((tm + CONV_PAD, cdim), F32)],
        compiler_params=_cp(("arbitrary",)),
        name="in_proj",
    )(h, norm_w, mod5, mod5, w_all, ssd_cw, ssd_cb, ml_cw, ml_cb)


def _ssd_kernel(p_ref, sm_ref, dtb_ref, alog_ref, dsk_ref, nw_ref, e8_ref, o_ref, st_ref):
    @pl.when(pl.program_id(0) == 0)
    def _():
        st_ref[...] = jnp.zeros(st_ref.shape, F32)

    nb = p_ref.shape[0]
    rng = range(nb)
    tri = _tri_mask()
    tri_bf = jnp.where(tri, 1.0, 0.0).astype(BF16)
    e8 = e8_ref[...]
    gw = SSD_GROUPS * SSD_STATE
    nh = SSD_HEADS // SSD_GROUPS
    hw = nh * SSD_HEAD_DIM
    lane = lax.broadcasted_iota(jnp.int32, (CHUNK, hw), 1)
    head_masks = [(lane >= hh * SSD_HEAD_DIM) & (lane < (hh + 1) * SSD_HEAD_DIM) for hh in range(nh)]

    neg_a = -jnp.exp(alog_ref[...])
    dt = [_softplus(sm_ref[b] + dtb_ref[...]) for b in rng]
    a_cum = _cumsum_rows_batched(tri_bf, [dt[b] * neg_a for b in rng])
    a_cum_t = [a_cum[b].T for b in rng]
    ex = _dot01_r2(jnp.concatenate([a_cum[b] for b in rng] + [dt[b] for b in rng], axis=0), e8)
    ac_e = [ex[b * CHUNK:(b + 1) * CHUNK] for b in rng]
    dt_e = [ex[(nb + b) * CHUNK:(nb + b + 1) * CHUNK] for b in rng]
    xs = [p_ref[b, :, D_BRANCH:2 * D_BRANCH].astype(F32) for b in rng]
    xd = [xs[b] * dt_e[b] for b in rng]
    xd_bf = [xd[b].astype(BF16) for b in rng]
    a_last_e = [ac_e[b][CHUNK - 1:CHUNK, :] for b in rng]
    xw_bf = [(xd[b] * jnp.exp(a_last_e[b] - ac_e[b])).astype(BF16) for b in rng]
    ea_e = [jnp.exp(ac_e[b]) for b in rng]

    ys = [[] for _ in rng]
    for g in range(SSD_GROUPS):
        gs = slice(g * hw, (g + 1) * hw)
        bg = [p_ref[b, :, 2 * D_BRANCH + g * SSD_STATE:2 * D_BRANCH + (g + 1) * SSD_STATE] for b in rng]
        cg = [p_ref[b, :, 2 * D_BRANCH + gw + g * SSD_STATE:2 * D_BRANCH + gw + (g + 1) * SSD_STATE]
              for b in rng]
        cb = [_dot_nt(cg[b], bg[b]) for b in rng]
        y_off = [_dot(cg[b], st_ref[b, g].astype(BF16)) for b in rng]
        m_cat, x_cat = [], []
        for b in rng:
            m_parts, x_parts = [], []
            for hh in range(nh):
                h = g * nh + hh
                seg = a_cum[b][:, h:h + 1] - a_cum_t[b][h:h + 1, :]
                dec = jnp.exp(jnp.where(tri, seg, NEG_BIG))
                m_parts.append((cb[b] * dec).astype(BF16))
                x_parts.append(jnp.where(head_masks[hh], xd_bf[b][:, gs], jnp.zeros((CHUNK, hw), BF16)))
            m_cat.append(jnp.concatenate(m_parts, axis=1))
            x_cat.append(jnp.concatenate(x_parts, axis=0))
        y_dg = [_dot(m_cat[b], x_cat[b]) for b in rng]
        s_loc = [_dot_tn(bg[b], xw_bf[b][:, gs]) for b in rng]
        for b in rng:
            ys[b].append(y_off[b] * ea_e[b][:, gs] + y_dg[b])
            st_ref[b, g] = st_ref[b, g] * jnp.exp(a_last_e[b][:, gs]) + s_loc[b]

    for b in rng:
        gate = p_ref[b, :, 0:D_BRANCH].astype(F32)
        y = jnp.concatenate(ys[b], axis=1) + dsk_ref[...] * xs[b]
        o_ref[b] = (_rms(y * gate) * nw_ref[...]).astype(o_ref.dtype)


def _ssd_call(l, p_ssd, p_sm, dt_bias, a_log, d_skip, norm_w, bsz, s):
    nc = s // CHUNK
    m = bsz * s
    e8 = np.zeros((LANES, D_BRANCH), np.float32)
    for h in range(SSD_HEADS):
        e8[h, h * SSD_HEAD_DIM:(h + 1) * SSD_HEAD_DIM] = 1.0
    row = lambda n: pl.BlockSpec((bsz, CHUNK, n), lambda c: (0, c, 0))
    return pl.pallas_call(
        _ssd_kernel,
        grid=(nc,),
        in_specs=[row(_W_SSD), row(LANES),
                  _layer_spec(l, (1, LANES), 2), _layer_spec(l, (1, LANES), 2),
                  _layer_spec(l, (1, D_BRANCH), 2), _layer_spec(l, (1, D_BRANCH), 2),
                  pl.BlockSpec((LANES, D_BRANCH), lambda c: (0, 0))],
        out_specs=row(D_BRANCH),
        out_shape=jax.ShapeDtypeStruct((bsz, s, D_BRANCH), BF16),
        scratch_shapes=[pltpu.VMEM((bsz, SSD_GROUPS, SSD_STATE, 256), F32)],
        compiler_params=_cp(("arbitrary",)),
        name="ssd_mixer",
    )(p_ssd.reshape(bsz, s, _W_SSD), p_sm.reshape(bsz, s, _W_SM), dt_bias, a_log, d_skip, norm_w,
      jnp.asarray(e8, BF16)).reshape(m, D_BRANCH)


def _mlstm_kernel(p_ref, sm_ref, ib_ref, fb_ref, nw_ref, jm_ref, o_ref, c_ref, m_ref):
    @pl.when(pl.program_id(0) == 0)
    def _():
        c_ref[...] = jnp.zeros(c_ref.shape, F32)
        m_ref[...] = jnp.zeros(m_ref.shape, F32)

    nb = p_ref.shape[0]
    rng = range(nb)
    d = D_BRANCH
    tri = _tri_mask()
    tri_bf = jnp.where(tri, 1.0, 0.0).astype(BF16)

    i_log = [sm_ref[b, :, LANES:2 * LANES] + ib_ref[...] for b in rng]
    f_log = [-_softplus(-(sm_ref[b, :, 2 * LANES:3 * LANES] + fb_ref[...])) for b in rng]
    f_cum = _cumsum_rows_batched(tri_bf, f_log)
    r = [i_log[b] - f_cum[b] for b in rng]
    r_t = [r[b].T for b in rng]
    a_col, inter, em, s_old, w_shift = [], [], [], [], []
    for b in rng:
        f_last = f_cum[b][CHUNK - 1:CHUNK, :]
        m_prev = m_ref[b]
        m_new = jnp.maximum(f_last + m_prev, jnp.max(f_last + r[b], axis=0, keepdims=True))
        s_old.append(jnp.exp(f_last + m_prev - m_new))
        w_shift.append(f_last - m_new)
        m_ref[b] = m_new
        g_inter = f_cum[b] + m_prev
        m_t = jnp.maximum(g_inter, f_cum[b] + _cummax_rows(r[b]))
        a_col.append(f_cum[b] - m_t)
        inter.append(jnp.exp(g_inter - m_t))
        em.append(jnp.exp(-m_t))

    ones = jnp.ones((CHUNK, ML_HEAD_DIM), BF16)
    ys = [[] for _ in rng]
    for h in range(ML_HEADS):
        c0 = h * ML_HEAD_DIM
        qh = [p_ref[b, :, d + c0:d + c0 + ML_HEAD_DIM] for b in rng]
        kh = [p_ref[b, :, 2 * d + c0:2 * d + c0 + ML_HEAD_DIM] for b in rng]
        v_ext = [jnp.concatenate([p_ref[b, :, 3 * d + c0:3 * d + c0 + ML_HEAD_DIM], ones], axis=1)
                 for b in rng]
        qk = [_dot_nt(qh[b], kh[b]) for b in rng]
        qc = [_dot(qh[b], c_ref[b, h].astype(BF16)) for b in rng]
        sm = []
        for b in rng:
            dmat = jnp.where(tri, a_col[b][:, h:h + 1] + r_t[b][h:h + 1, :], NEG_BIG)
            sm.append((qk[b] * jnp.exp(dmat)).astype(BF16))
        numden = [_dot(sm[b], v_ext[b]) + inter[b][:, h:h + 1] * qc[b] for b in rng]
        kw_t = [(kh[b].T.astype(F32) * jnp.exp(r_t[b][h:h + 1, :] + w_shift[b][:, h:h + 1])).astype(BF16)
                for b in rng]
        upd = [_dot(kw_t[b], v_ext[b]) for b in rng]
        for b in rng:
            num = numden[b][:, 0:ML_HEAD_DIM]
            den = numden[b][:, ML_HEAD_DIM:]
            o_gate = p_ref[b, :, 4 * d + c0:4 * d + c0 + ML_HEAD_DIM].astype(F32)
            ys[b].append(o_gate * (num / jnp.maximum(jnp.abs(den), em[b][:, h:h + 1])))
            c_ref[b, h] = s_old[b][:, h:h + 1] * c_ref[b, h] + upd[b]

    y = [jnp.concatenate(ys[b], axis=1) for b in rng]
    ms = [_dot((y[b] * y[b]).astype(BF16), jm_ref[...]) for b in rng]
    for b in rng:
        gate = p_ref[b, :, 0:d].astype(F32)
        o_ref[b] = (y[b] * lax.rsqrt(ms[b] + NORM_EPS) * nw_ref[...] * gate).astype(o_ref.dtype)


def _mlstm_call(l, p_ml, p_sm, i_bias, f_bias, norm_w, bsz, s):
    nc = s // CHUNK
    m = bsz * s
    row = lambda n: pl.BlockSpec((bsz, CHUNK, n), lambda c: (0, c, 0))
    jm = np.kron(np.eye(ML_HEADS), np.full((ML_HEAD_DIM, ML_HEAD_DIM), 1.0 / ML_HEAD_DIM))
    return pl.pallas_call(
        _mlstm_kernel,
        grid=(nc,),
        in_specs=[row(_W_ML), row(_W_SM),
                  _layer_spec(l, (1, LANES), 2), _layer_spec(l, (1, LANES), 2),
                  _layer_spec(l, (1, D_BRANCH), 2),
                  pl.BlockSpec((D_BRANCH, D_BRANCH), lambda c: (0, 0))],
        out_specs=row(D_BRANCH),
        out_shape=jax.ShapeDtypeStruct((bsz, s, D_BRANCH), BF16),
        scratch_shapes=[pltpu.VMEM((bsz, ML_HEADS, ML_HEAD_DIM, 2 * ML_HEAD_DIM), F32),
                        pltpu.VMEM((bsz, 1, LANES), F32)],
        compiler_params=_cp(("arbitrary",)),
        name="mlstm_mixer",
    )(p_ml.reshape(bsz, s, _W_ML), p_sm.reshape(bsz, s, _W_SM), i_bias, f_bias,
      norm_w, jnp.asarray(jm, BF16)).reshape(m, D_BRANCH)


def _ret_log_gamma():
    return [math.log1p(-2.0 ** (-(RET_DECAY_BASE + h))) for h in range(RET_HEADS)]


def _ret_consts():
    lg = _ret_log_gamma()
    pos = np.arange(CHUNK, dtype=np.float64)
    rel = pos[:, None] - pos[None, :]
    dm = np.stack([np.where(rel >= 0, np.exp(np.maximum(rel, 0.0) * g), 0.0) for g in lg])
    from_start = np.concatenate(
        [np.repeat(np.exp((pos + 1.0) * g)[:, None], RET_V, axis=1) for g in lg], axis=1)
    to_end = np.concatenate(
        [np.repeat(np.exp((CHUNK - 1.0 - pos) * g)[:, None], RET_QK, axis=1) for g in lg], axis=1)
    qw, vw = RET_HEADS * RET_QK, RET_HEADS * RET_V
    chunk_decay = np.zeros((qw, vw))
    for h, g in enumerate(lg):
        chunk_decay[h * RET_QK:(h + 1) * RET_QK, h * RET_V:(h + 1) * RET_V] = math.exp(CHUNK * g)
    block_diag = (chunk_decay > 0).astype(np.float64)
    swap = np.zeros((qw, qw))
    for j in range(qw):
        base, off = (j // RET_QK) * RET_QK, j % RET_QK
        swap[base + (off + RET_QK // 2) % RET_QK, j] = 1.0
    f = lambda a: jnp.asarray(a, F32)
    return f(dm), f(from_start), f(to_end), f(chunk_decay), f(block_diag), jnp.asarray(swap, BF16)


def _ret_kernel(p_ref, cos_ref, sin_ref, dm_ref, fs_ref, te_ref, cd_ref, bd_ref, sw_ref, nw_ref,
                o_ref, r_ref):
    @pl.when(pl.program_id(0) == 0)
    def _():
        r_ref[...] = jnp.zeros(r_ref.shape, F32)

    nb = p_ref.shape[0]
    d = D_BRANCH
    qw = RET_HEADS * RET_QK
    sw = sw_ref[...]
    lane = lax.broadcasted_iota(jnp.int32, (CHUNK, qw), 1)
    head_masks = [(lane >= h * RET_QK) & (lane < (h + 1) * RET_QK) for h in range(RET_HEADS)]

    def v_of(b):
        return p_ref[b, :, d + 2 * qw:2 * d + 2 * qw]

    q_raw = [p_ref[b, :, d:d + qw] for b in range(nb)]
    k_raw = [p_ref[b, :, d + qw:d + 2 * qw] for b in range(nb)]
    q_sw = [_dot(q_raw[b], sw) for b in range(nb)]
    k_sw = [_dot(k_raw[b], sw) for b in range(nb)]
    qs, ks = [], []
    for b in range(nb):
        cos_e = jnp.concatenate([cos_ref[b]] * (qw // LANES), axis=1)
        sin_e = jnp.concatenate([sin_ref[b]] * (qw // LANES), axis=1)
        qs.append(q_raw[b].astype(F32) * cos_e + q_sw[b] * sin_e)
        ks.append((k_raw[b].astype(F32) * cos_e + k_sw[b] * sin_e) * (RET_QK ** -0.5))
    k_bf = [k.astype(BF16) for k in ks]
    ys = [_dot(qs[b].astype(BF16), r_ref[b].astype(BF16)) * fs_ref[...] for b in range(nb)]
    inner = [[] for _ in range(nb)]
    for h in range(RET_HEADS):
        scores = [(_dot_nt(jnp.where(head_masks[h], qs[b], jnp.zeros_like(qs[b])).astype(BF16), k_bf[b])
                   * dm_ref[h]).astype(BF16) for b in range(nb)]
        for b in range(nb):
            inner[b].append(_dot(scores[b], v_of(b)[:, h * RET_V:(h + 1) * RET_V]))
    ys = [ys[b] + jnp.concatenate(inner[b], axis=1) for b in range(nb)]
    upd = [_dot_tn((ks[b] * te_ref[...]).astype(BF16), v_of(b)) for b in range(nb)]
    for b in range(nb):
        r_ref[b] = r_ref[b] * cd_ref[...] + upd[b] * bd_ref[...]
    for b in range(nb):
        gate = p_ref[b, :, 0:d].astype(F32)
        outs = [_rms(ys[b][:, h * RET_V:(h + 1) * RET_V]) for h in range(RET_HEADS)]
        o_ref[b] = (jnp.concatenate(outs, axis=1) * nw_ref[...] * gate).astype(o_ref.dtype)


def _ret_call(l, p_ret, cos_t, sin_t, consts, norm_w, bsz, s):
    nc = s // CHUNK
    m = bsz * s
    qw, vw = RET_HEADS * RET_QK, RET_HEADS * RET_V
    dm, fs, te, cd, bd, sw = consts
    const = lambda shape: pl.BlockSpec(shape, lambda c: (0,) * len(shape))
    row = lambda n: pl.BlockSpec((bsz, CHUNK, n), lambda c: (0, c, 0))
    return pl.pallas_call(
        _ret_kernel,
        grid=(nc,),
        in_specs=[row(_W_RET), row(LANES), row(LANES),
                  const((RET_HEADS, CHUNK, CHUNK)), const((CHUNK, vw)), const((CHUNK, qw)),
                  const((qw, vw)), const((qw, vw)), const((qw, qw)),
                  _layer_spec(l, (1, D_BRANCH), 2)],
        out_specs=row(D_BRANCH),
        out_shape=jax.ShapeDtypeStruct((bsz, s, D_BRANCH), BF16),
        scratch_shapes=[pltpu.VMEM((bsz, qw, vw), F32)],
        compiler_params=_cp(("arbitrary",)),
        name="retention_mixer",
    )(p_ret.reshape(bsz, s, _W_RET), cos_t.reshape(bsz, s, LANES), sin_t.reshape(bsz, s, LANES),
      dm, fs, te, cd, bd, sw, norm_w).reshape(m, D_BRANCH)


def _s5_compact(lam_re, lam_im, b_re, b_im, c_re, c_im, log_step):
    depth = lam_re.shape[0]
    nb, ngl, sub = S5_LANE_BLOCKS, S5_LOCAL_GROUPS, S5_SUB
    step = jnp.exp(log_step.astype(F32))[..., None]
    lr = jnp.minimum(lam_re.astype(F32), -1e-4)
    li = lam_im.astype(F32)
    mag = jnp.exp(lr * step)
    ang = li * step
    ab_re = mag * jnp.cos(ang)
    ab_im = mag * jnp.sin(ang)
    den = lr * lr + li * li
    coef_re = ((ab_re - 1.0) * lr + ab_im * li) / den
    coef_im = (ab_im * lr - (ab_re - 1.0) * li) / den
    br_t, bi_t = jnp.swapaxes(b_re.astype(F32), -1, -2), jnp.swapaxes(b_im.astype(F32), -1, -2)
    cf_re, cf_im = coef_re[:, :, None, :], coef_im[:, :, None, :]
    bb_re = cf_re * br_t - cf_im * bi_t
    bb_im = cf_re * bi_t + cf_im * br_t
    pw_re, pw_im = [jnp.ones_like(ab_re)], [jnp.zeros_like(ab_im)]
    for _ in range(sub):
        pr, pi = pw_re[-1], pw_im[-1]
        pw_re.append(pr * ab_re - pi * ab_im)
        pw_im.append(pr * ab_im + pi * ab_re)
    cr, ci = c_re.astype(F32), c_im.astype(F32)
    cr_t, ci_t = jnp.swapaxes(cr, -1, -2), jnp.swapaxes(ci, -1, -2)
    hp = lax.Precision.HIGHEST
    abb_re = [pw_re[k][:, :, None, :] * bb_re - pw_im[k][:, :, None, :] * bb_im for k in range(sub)]
    abb_im = [pw_re[k][:, :, None, :] * bb_im + pw_im[k][:, :, None, :] * bb_re for k in range(sub)]
    taps_t = [jnp.einsum('lgip,lgop->lgio', abb_re[k], cr, precision=hp)
              - jnp.einsum('lgip,lgop->lgio', abb_im[k], ci, precision=hp) for k in range(sub)]

    def by_block(rows):
        a = jnp.stack(rows).reshape(len(rows), depth, nb, ngl, rows[0].shape[-2], rows[0].shape[-1])
        return a.transpose(1, 2, 0, 3, 4, 5).reshape(depth, nb, -1, rows[0].shape[-1])

    c_in = jnp.concatenate([by_block([abb_re[sub - 1 - t] for t in range(sub)]),
                            by_block([abb_im[sub - 1 - t] for t in range(sub)])], axis=-1)
    zero = jnp.zeros_like(taps_t[0])
    toe = by_block([jnp.concatenate([zero] * t + taps_t[:sub - t], axis=-1) for t in range(sub)])

    def rows_x(c_t, s_t, sign):
        cols = [c_t * pw_re[k][..., None] - sign * s_t * pw_im[k][..., None] for k in range(1, sub + 1)]
        return jnp.stack(cols, axis=-2).reshape(depth, nb, ngl * S5_STATE, sub * S5_GROUP)

    x_re = rows_x(cr_t, ci_t, 1.0)
    x_im = -(jnp.stack([cr_t * pw_im[k][..., None] + ci_t * pw_re[k][..., None]
                        for k in range(1, sub + 1)], axis=-2)
             .reshape(depth, nb, ngl * S5_STATE, sub * S5_GROUP))
    c_out = jnp.concatenate([toe, x_re, x_im], axis=2)
    a8 = jnp.concatenate([pw_re[sub].reshape(depth, nb, 1, ngl * S5_STATE),
                          pw_im[sub].reshape(depth, nb, 1, ngl * S5_STATE)], axis=-1)
    return c_in.astype(BF16), c_out.astype(BF16), a8


def _s5_expanders():
    kin = S5_SUB * LANES
    half = S5_LOCAL_GROUPS * S5_STATE
    sel_in = np.zeros((LANES, 2 * half), np.float32)
    csel_in = np.zeros((LANES, 2 * half), np.float32)
    for col in range(2 * half):
        sel_in[(col // half) * S5_STATE + col % S5_STATE, col] = 1.0
        csel_in[(col % half) // S5_STATE, col] = 1.0
    sel_out = np.zeros((LANES, kin), np.float32)
    csel_out = np.zeros((LANES, kin), np.float32)
    for col in range(kin):
        sel_out[(col // LANES) * S5_GROUP + col % S5_GROUP, col] = 1.0
        csel_out[(col // S5_GROUP) % S5_LOCAL_GROUPS, col] = 1.0
    rsel_in = np.zeros((kin, LANES), np.float32)
    for r in range(kin):
        rsel_in[r, (r // S5_GROUP) % S5_LOCAL_GROUPS] = 1.0
    rsel_out = np.zeros((2 * kin, LANES), np.float32)
    rsel_out[:kin] = rsel_in
    for r in range(kin):
        rsel_out[kin + r, (r % half) // S5_STATE] = 1.0
    return tuple(jnp.asarray(a, BF16) for a in (sel_in, csel_in, rsel_in, sel_out, csel_out, rsel_out))


def _s5_fold(a, bsz, s):
    j = s // S5_SUB
    a = a.reshape(bsz, j, S5_SUB, S5_LANE_BLOCKS, LANES).transpose(1, 0, 3, 2, 4)
    return a.reshape(j * bsz, S5_LANE_BLOCKS * S5_SUB * LANES)


def _s5_unfold(a, bsz, s):
    j = s // S5_SUB
    a = a.reshape(j, bsz, S5_LANE_BLOCKS, S5_SUB, LANES).transpose(1, 0, 3, 2, 4)
    return a.reshape(bsz * s, D_BRANCH)


S5_ROW_BLOCK = 512


def _s5_scan_kernel(u_ref, cin_ref, cout_ref, a8_ref, sin_ref, csin_ref, rsin_ref,
                    sout_ref, csout_ref, rsout_ref, o_ref, x_ref, min_ref, mout_ref, *, bsz):
    rows = u_ref.shape[0]
    half = S5_LOCAL_GROUPS * S5_STATE
    rb = min(S5_ROW_BLOCK, rows)

    min_ref[...] = (_dot(cin_ref[...], sin_ref[...]) * _dot(rsin_ref[...], csin_ref[...])).astype(BF16)
    mout_ref[...] = (_dot(cout_ref[...], sout_ref[...]) * _dot(rsout_ref[...], csout_ref[...])).astype(BF16)

    for r0 in range(0, rows, rb):
        x_ref[r0:r0 + rb, :] = _dot(u_ref[r0:r0 + rb, :], min_ref[...])

    a_re = a8_ref[:, 0:half]
    a_im = a8_ref[:, half:]

    def body(i, carry):
        x_re, x_im = carry
        r0 = pl.multiple_of(i * bsz, bsz)
        loc = x_ref[pl.ds(r0, bsz), :]
        x_ref[pl.ds(r0, bsz), :] = jnp.concatenate([x_re, x_im], axis=1)
        n_re = a_re * x_re - a_im * x_im + loc[:, 0:half]
        n_im = a_re * x_im + a_im * x_re + loc[:, half:]
        return n_re, n_im

    zero = jnp.zeros((bsz, half), F32)
    lax.fori_loop(0, rows // bsz, body, (zero, zero), unroll=8)

    kin = u_ref.shape[1]
    nt_w = 2 * LANES
    for r0 in range(0, rows, rb):
        xb = x_ref[r0:r0 + rb, :].astype(BF16)
        for c0 in range(0, kin, nt_w):
            k_u = c0 + nt_w
            y = _dot(u_ref[r0:r0 + rb, 0:k_u], mout_ref[0:k_u, c0:c0 + nt_w])
            y = y + _dot(xb, mout_ref[kin:2 * kin, c0:c0 + nt_w])
            o_ref[r0:r0 + rb, c0:c0 + nt_w] = y.astype(o_ref.dtype)


def _s5_scan_call(l, u_fold, c_in, c_out, a8, expanders, bsz):
    rows = u_fold.shape[0]
    kin = S5_SUB * LANES
    lspec = lambda shape: pl.BlockSpec((None, None) + shape, lambda lb: (l, lb, 0, 0))
    const = lambda a: pl.BlockSpec(a.shape, lambda lb: (0, 0))
    sel_in, csel_in, rsel_in, sel_out, csel_out, rsel_out = expanders
    return pl.pallas_call(
        functools.partial(_s5_scan_kernel, bsz=bsz),
        grid=(S5_LANE_BLOCKS,),
        in_specs=[pl.BlockSpec((rows, kin), lambda lb: (0, lb)),
                  lspec((kin, LANES)), lspec((2 * kin, LANES)), lspec((1, kin)),
                  const(sel_in), const(csel_in), const(rsel_in),
                  const(sel_out), const(csel_out), const(rsel_out)],
        out_specs=pl.BlockSpec((rows, kin), lambda lb: (0, lb)),
        out_shape=jax.ShapeDtypeStruct(u_fold.shape, BF16),
        scratch_shapes=[pltpu.VMEM((rows, kin), F32),
                        pltpu.VMEM((kin, kin), BF16), pltpu.VMEM((2 * kin, kin), BF16)],
        compiler_params=_cp(("parallel",)),
        name="s5_scan",
    )(u_fold, c_in, c_out, a8, sel_in, csel_in, rsel_in, sel_out, csel_out, rsel_out)


def _outproj_kernel(y_ssd, y_ml, s5g_ref, s5u_ref, s5y_ref, y_ret, w_ref, h_ref, g_ref, fw_ref,
                    d_ref, wg_ref, bg_ref, nw5_ref, o_ref, *, final_norm):
    d = D_BRANCH
    u = s5u_ref[...].astype(F32)
    y = s5y_ref[...].astype(F32) + d_ref[...] * u
    y = 0.5 * y * (1.0 + jnp.tanh(math.sqrt(2.0 / math.pi) * (y + 0.044715 * (y * y * y))))
    g = _dot(y.astype(BF16), wg_ref[...]) + bg_ref[...]
    y = g[:, 0:d] * _sigmoid(g[:, d:2 * d])
    y_s5 = (_rms(y) * nw5_ref[...] * s5g_ref[...].astype(F32)).astype(BF16)

    acc = _dot(y_ssd[...], w_ref[0 * d:1 * d, :])
    acc = acc + _dot(y_ml[...], w_ref[1 * d:2 * d, :])
    acc = acc + _dot(y_s5, w_ref[2 * d:3 * d, :])
    acc = acc + _dot(y_ret[...], w_ref[3 * d:4 * d, :])
    h = h_ref[...] + g_ref[...] * acc
    if final_norm:
        h = _rms(h) * fw_ref[...]
    o_ref[...] = h


def _outproj_call(l, y_ssd, y_ml, p_s5g, p_s5u, y_ssm, y_ret, w_out, h, mod5, final_w,
                  s5_d, w_glu, b_glu, s5_nw, s, final_norm):
    m = h.shape[0]
    tm = min(ROW_TILE, s)
    per_batch = s // tm
    yspec = pl.BlockSpec((tm, D_BRANCH), lambda i: (i, 0))
    return pl.pallas_call(
        functools.partial(_outproj_kernel, final_norm=final_norm),
        grid=(m // tm,),
        in_specs=[yspec] * 6 + [
            _layer_spec(l, (4 * D_BRANCH, D_MODEL), 1),
            pl.BlockSpec((tm, D_MODEL), lambda i: (i, 0)),
            pl.BlockSpec((None, None, None, 1, D_MODEL), lambda i: (l, i // per_batch, 2, 0, 0)),
            pl.BlockSpec((1, D_MODEL), lambda i: (0, 0)),
            _layer_spec(l, (1, D_BRANCH), 1), _layer_spec(l, (D_BRANCH, 2 * D_BRANCH), 1),
            _layer_spec(l, (1, 2 * D_BRANCH), 1), _layer_spec(l, (1, D_BRANCH), 1),
        ],
        out_specs=pl.BlockSpec((tm, D_MODEL), lambda i: (i, 0)),
        out_shape=jax.ShapeDtypeStruct((m, D_MODEL), F32),
        compiler_params=_cp(("parallel",)),
        name="out_proj",
    )(y_ssd, y_ml, p_s5g, p_s5u, y_ssm, y_ret, w_out, h, mod5, final_w, s5_d, w_glu, b_glu, s5_nw)


def _pack_w_in(w_in):
    assert w_in.shape[-1] == _D_IN
    pad = lambda a: jnp.pad(a, [(0, 0)] * (a.ndim - 1) + [(0, LANES - a.shape[-1])])
    parts = [w_in[..., _O_SSD_Z:_O_SSD_DT], w_in[..., _O_ML_Z:_O_ML_I], w_in[..., _O_S5_Z:_O_RET_Z],
             w_in[..., _O_RET_Z:_D_IN], pad(w_in[..., _O_SSD_DT:_O_ML_Z]),
             pad(w_in[..., _O_ML_I:_O_ML_F]), pad(w_in[..., _O_ML_F:_O_S5_Z])]
    return jnp.concatenate(parts, axis=-1).astype(BF16)


def _pad_lanes(v):
    return jnp.pad(v.astype(F32), ((0, 0), (0, LANES - v.shape[-1])))[:, None, :]


def kernel(x, c, positions, norm_w, w_ada, b_ada, w_in, w_out, ssd_conv_w, ssd_conv_b, ssd_dt_bias, ssd_a_log, ssd_d, ssd_norm_w, ml_conv_w, ml_conv_b, ml_i_bias, ml_f_bias, ml_norm_w, s5_lambda_re, s5_lambda_im, s5_b_re, s5_b_im, s5_c_re, s5_c_im, s5_d, s5_log_step, s5_w_glu, s5_b_glu, s5_norm_w, ret_norm_w, final_norm_w):
    bsz, s, d = x.shape
    depth = w_in.shape[0]
    assert d == D_MODEL and s % CHUNK == 0 and bsz % 8 == 0
    m = bsz * s
    row3 = lambda a: a.astype(F32)[:, None, :]

    mod = _mod_call(c.astype(F32), w_ada, b_ada)
    mod5 = mod.reshape(depth, bsz, 3, 1, D_MODEL)
    cos_t, sin_t = _rope_call(positions)
    ret_consts = _ret_consts()
    w_all = _pack_w_in(w_in)
    w_out_bf = w_out.astype(BF16)
    w_glu_bf = s5_w_glu.astype(BF16)
    c_in, c_out, a8 = _s5_compact(s5_lambda_re, s5_lambda_im, s5_b_re, s5_b_im, s5_c_re, s5_c_im,
                                  s5_log_step)
    s5_exp = _s5_expanders()
    norm_w3, final_w = row3(norm_w), final_norm_w.reshape(1, D_MODEL).astype(F32)
    ssd_cb3, ml_cb3 = row3(ssd_conv_b), row3(ml_conv_b)
    dt_bias3, a_log3 = _pad_lanes(ssd_dt_bias), _pad_lanes(ssd_a_log)
    d_skip3 = row3(jnp.repeat(ssd_d, SSD_HEAD_DIM, axis=-1))
    i_bias3, f_bias3 = _pad_lanes(ml_i_bias), _pad_lanes(ml_f_bias)
    ssd_nw3, ml_nw3, s5_nw3, ret_nw3 = row3(ssd_norm_w), row3(ml_norm_w), row3(s5_norm_w), row3(ret_norm_w)
    s5_d3, b_glu3 = row3(s5_d), row3(s5_b_glu)

    h = x.reshape(m, D_MODEL)
    for l in range(depth):
        p_ssd, p_ml, p_s5g, p_s5u, p_ret, p_sm = _inproj_call(
            l, h, norm_w3, mod5, w_all, ssd_conv_w, ssd_cb3, ml_conv_w, ml_cb3, s)
        y_ssd = _ssd_call(l, p_ssd, p_sm, dt_bias3, a_log3, d_skip3, ssd_nw3, bsz, s)
        y_ml = _mlstm_call(l, p_ml, p_sm, i_bias3, f_bias3, ml_nw3, bsz, s)
        y_fold = _s5_scan_call(l, _s5_fold(p_s5u, bsz, s), c_in, c_out, a8, s5_exp, bsz)
        y_ssm = _s5_unfold(y_fold, bsz, s)
        y_ret = _ret_call(l, p_ret, cos_t, sin_t, ret_consts, ret_nw3, bsz, s)
        h = _outproj_call(l, y_ssd, y_ml, p_s5g, p_s5u, y_ssm, y_ret, w_out_bf, h, mod5, final_w,
                          s5_d3, w_glu_bf, b_glu3, s5_nw3, s, final_norm=(l == depth - 1))
    return h.reshape(bsz, s, D_MODEL)
```

```python
import functools
import math

import numpy as np
import jax
import jax.numpy as jnp
from jax import lax
from jax.experimental import pallas as pl
from jax.experimental.pallas import tpu as pltpu

F32 = jnp.float32
BF16 = jnp.bfloat16

D_MODEL = 1024
D_BRANCH = 512
CHUNK = 128
CONV_K = 4
CONV_PAD = 8
NORM_EPS = 1e-6
NEG_BIG = -1e30

SSD_HEADS = 8
SSD_HEAD_DIM = 64
SSD_GROUPS = 2
SSD_STATE = 128
ML_HEADS = 4
ML_HEAD_DIM = 128
S5_GROUPS = 32
S5_GROUP = 16
S5_STATE = 64
S5_SUB = 8
S5_LANE_BLOCKS = 4
S5_LOCAL_GROUPS = S5_GROUPS // S5_LANE_BLOCKS
RET_HEADS = 4
RET_QK = 64
RET_V = 128
RET_DECAY_BASE = 5.0
ROPE_BASE = 10000.0
LANES = 128

_O_SSD_Z, _O_SSD_DT = 0, 1536
_O_ML_Z, _O_ML_I, _O_ML_F = 1544, 4104, 4108
_O_S5_Z = 4112
_O_RET_Z = 5136
_D_IN = 6672
_W_SSD, _W_ML, _W_S5, _W_RET, _W_SM = 1536, 2560, 1024, 1536, 384
_W_ALL = _W_SSD + _W_ML + _W_S5 + _W_RET + _W_SM
_W_S5G = _W_S5 - D_BRANCH

VMEM_LIMIT = 56 * 1024 * 1024
ROW_TILE = 512


def _cp(sem):
    return pltpu.CompilerParams(dimension_semantics=sem, vmem_limit_bytes=VMEM_LIMIT)


def _layer_spec(l, shape, nargs):
    return pl.BlockSpec((None,) + tuple(shape), lambda *_: (l,) + (0,) * len(shape))


def _dot(a, b):
    return jnp.dot(a, b, preferred_element_type=F32)


def _dot_nt(a, b):
    return lax.dot_general(a, b, (((1,), (1,)), ((), ())), preferred_element_type=F32)


def _dot_tn(a, b):
    return lax.dot_general(a, b, (((0,), (0,)), ((), ())), preferred_element_type=F32)


def _sigmoid(x):
    return 0.5 + 0.5 * jnp.tanh(0.5 * x)


def _silu(x):
    hx = 0.5 * x
    return hx + hx * jnp.tanh(hx)


def _softplus(x):
    return jnp.maximum(x, 0.0) + jnp.log(1.0 + jnp.exp(-jnp.abs(x)))


def _split3(v):
    hi = v.astype(BF16)
    r1 = v - hi.astype(F32)
    mid = r1.astype(BF16)
    lo = (r1 - mid.astype(F32)).astype(BF16)
    return hi, mid, lo


def _dot01_r2(v, m01):
    hi = v.astype(BF16)
    lo = (v - hi.astype(F32)).astype(BF16)
    return _dot(hi, m01) + _dot(lo, m01)


def _cumsum_rows_batched(tri_bf, xs):
    terms = []
    for x in xs:
        terms.extend(_split3(x))
    res = _dot(tri_bf, jnp.concatenate(terms, axis=1))
    w = xs[0].shape[1]
    return [res[:, (3 * i) * w:(3 * i + 1) * w] + res[:, (3 * i + 1) * w:(3 * i + 2) * w]
            + res[:, (3 * i + 2) * w:(3 * i + 3) * w] for i in range(len(xs))]


def _cummax_rows(x):
    n = x.shape[0]
    row = lax.broadcasted_iota(jnp.int32, x.shape, 0)
    sh = 1
    while sh < n:
        x = jnp.maximum(x, jnp.where(row >= sh, pltpu.roll(x, sh, axis=0), NEG_BIG))
        sh *= 2
    return x


def _tri_mask():
    row = lax.broadcasted_iota(jnp.int32, (CHUNK, CHUNK), 0)
    col = lax.broadcasted_iota(jnp.int32, (CHUNK, CHUNK), 1)
    return row >= col


def _rms(y):
    return y * lax.rsqrt(jnp.mean(y * y, axis=-1, keepdims=True) + NORM_EPS)


def _mod_kernel(c_ref, w_ref, b_ref, o_ref):
    cond = _silu(c_ref[...])
    o_ref[0] = _dot(cond.astype(BF16), w_ref[0].astype(BF16)) + b_ref[0]


def _mod_call(c, w_ada, b_ada):
    depth, d, n = w_ada.shape
    bsz = c.shape[0]
    tn = 1024
    return pl.pallas_call(
        _mod_kernel,
        grid=(depth, n // tn),
        in_specs=[
            pl.BlockSpec((bsz, d), lambda l, j: (0, 0)),
            pl.BlockSpec((1, d, tn), lambda l, j: (l, 0, j)),
            pl.BlockSpec((1, 1, tn), lambda l, j: (l, 0, j)),
        ],
        out_specs=pl.BlockSpec((1, bsz, tn), lambda l, j: (l, 0, j)),
        out_shape=jax.ShapeDtypeStruct((depth, bsz, n), F32),
        compiler_params=_cp(("parallel", "parallel")),
        name="adaln_mod",
    )(c, w_ada, b_ada.reshape(depth, 1, n))


def _rope_kernel(pos_ref, f_ref, sg_ref, cos_ref, sin_ref):
    ang = pos_ref[...] * f_ref[...]
    cos_ref[...] = jnp.cos(ang)
    sin_ref[...] = jnp.sin(ang) * sg_ref[...]


def _rope_call(positions):
    bsz, s = positions.shape
    m = bsz * s
    half = RET_QK // 2
    inv_freq = jnp.exp(-math.log(ROPE_BASE) * jnp.arange(half, dtype=F32) / half)
    f128 = jnp.tile(inv_freq, LANES // half).reshape(1, LANES)
    sign = np.tile(np.concatenate([-np.ones(half), np.ones(half)]), LANES // RET_QK)
    sign = jnp.asarray(sign.reshape(1, LANES), F32)
    pos = jnp.broadcast_to(positions.astype(F32).reshape(m, 1), (m, LANES))
    tm = min(1024, m)
    return pl.pallas_call(
        _rope_kernel,
        grid=(m // tm,),
        in_specs=[
            pl.BlockSpec((tm, LANES), lambda i: (i, 0)),
            pl.BlockSpec((1, LANES), lambda i: (0, 0)),
            pl.BlockSpec((1, LANES), lambda i: (0, 0)),
        ],
        out_specs=[pl.BlockSpec((tm, LANES), lambda i: (i, 0))] * 2,
        out_shape=[jax.ShapeDtypeStruct((m, LANES), F32)] * 2,
        compiler_params=_cp(("parallel",)),
        name="rope_tables",
    )(pos, f128, sign)


def _conv_silu(x, buf_ref, cw_ref, cb_ref):
    n = x.shape[0]
    buf_ref[CONV_PAD:CONV_PAD + n, :] = x
    acc = cb_ref[...] + cw_ref[CONV_K - 1:CONV_K, :] * x
    for k in range(CONV_K - 1):
        off = CONV_PAD - (CONV_K - 1) + k
        acc = acc + cw_ref[k:k + 1, :] * buf_ref[off:off + n, :]
    buf_ref[0:CONV_PAD, :] = x[n - CONV_PAD:n, :]
    return _silu(acc)


def _inproj_kernel(h_ref, nw_ref, sc_ref, sh_ref, w_ref, scw_ref, scb_ref, mcw_ref, mcb_ref,
                   o_ssd, o_ml, o_s5g, o_s5u, o_ret, o_sm, sbuf_ref, mbuf_ref, *, per_batch):
    @pl.when(pl.program_id(0) % per_batch == 0)
    def _():
        sbuf_ref[0:CONV_PAD, :] = jnp.zeros((CONV_PAD, sbuf_ref.shape[1]), F32)
        mbuf_ref[0:CONV_PAD, :] = jnp.zeros((CONV_PAD, mbuf_ref.shape[1]), F32)

    x = h_ref[...]
    hn = _rms(x) * nw_ref[...]
    hb = (hn * (1.0 + sc_ref[...]) + sh_ref[...]).astype(BF16)
    d = D_BRANCH

    def proj(col, width):
        return _dot(hb, w_ref[:, col:col + width])

    col = 0
    o_ssd[:, 0:d] = _silu(proj(col, d)).astype(BF16)
    o_ssd[:, d:3 * d] = _conv_silu(proj(col + d, 2 * d), sbuf_ref, scw_ref, scb_ref).astype(BF16)
    col += _W_SSD
    o_ml[:, 0:d] = _silu(proj(col, d)).astype(BF16)
    qk = _conv_silu(proj(col + d, 2 * d), mbuf_ref, mcw_ref, mcb_ref)
    o_ml[:, d:2 * d] = qk[:, 0:d].astype(BF16)
    o_ml[:, 2 * d:3 * d] = (qk[:, d:2 * d] * (ML_HEAD_DIM ** -0.5)).astype(BF16)
    o_ml[:, 3 * d:4 * d] = proj(col + 3 * d, d).astype(BF16)
    o_ml[:, 4 * d:5 * d] = _sigmoid(proj(col + 4 * d, d)).astype(BF16)
    col += _W_ML
    o_s5g[...] = _silu(proj(col, d)).astype(BF16)
    o_s5u[...] = proj(col + d, d).astype(BF16)
    col += _W_S5
    o_ret[:, 0:d] = _silu(proj(col, d)).astype(BF16)
    o_ret[:, d:3 * d] = proj(col + d, 2 * d).astype(BF16)
    col += _W_RET
    o_sm[...] = proj(col, _W_SM)


def _inproj_call(l, h, norm_w, mod5, w_all, ssd_cw, ssd_cb, ml_cw, ml_cb, s):
    m = h.shape[0]
    tm = min(ROW_TILE, s)
    per_batch = s // tm
    widths = (_W_SSD, _W_ML, _W_S5G, D_BRANCH, _W_RET, _W_SM)
    dtypes = (BF16, BF16, BF16, BF16, BF16, F32)
    cdim = 2 * D_BRANCH
    mod_spec = lambda which: pl.BlockSpec((None, None, None, 1, D_MODEL),
                                          lambda i: (l, i // per_batch, which, 0, 0))
    return pl.pallas_call(
        functools.partial(_inproj_kernel, per_batch=per_batch),
        grid=(m // tm,),
        in_specs=[
            pl.BlockSpec((tm, D_MODEL), lambda i: (i, 0)),
            _layer_spec(l, (1, D_MODEL), 1),
            mod_spec(1), mod_spec(0),
            pl.BlockSpec((None, D_MODEL, _W_ALL), lambda i: (l, 0, 0), pipeline_mode=pl.Buffered(1)),
            _layer_spec(l, (CONV_K, cdim), 1), _layer_spec(l, (1, cdim), 1),
            _layer_spec(l, (CONV_K, cdim), 1), _layer_spec(l, (1, cdim), 1),
        ],
        out_specs=[pl.BlockSpec((tm, n), lambda i: (i, 0)) for n in widths],
        out_shape=[jax.ShapeDtypeStruct((m, n), dt) for n, dt in zip(widths, dtypes)],
        scratch_shapes=[pltpu.VMEM((tm + CONV_PAD, cdim), F32), pltpu.VMEM((tm + CONV_PAD, cdim), F32)],
        compiler_params=_cp(("arbitrary",)),
        name="in_proj",
    )(h, norm_w, mod5, mod5, w_all, ssd_cw, ssd_cb, ml_cw, ml_cb)


def _ssd_kernel(p_ref, sm_ref, dtb_ref, alog_ref, dsk_ref, nw_ref, e8_ref, o_ref, st_ref):
    @pl.when(pl.program_id(0) == 0)
    def _():
        st_ref[...] = jnp.zeros(st_ref.shape, F32)

    nb = p_ref.shape[0]
    rng = range(nb)
    tri = _tri_mask()
    tri_bf = jnp.where(tri, 1.0, 0.0).astype(BF16)
    e8 = e8_ref[...]
    gw = SSD_GROUPS * SSD_STATE
    nh = SSD_HEADS // SSD_GROUPS
    hw = nh * SSD_HEAD_DIM
    lane = lax.broadcasted_iota(jnp.int32, (CHUNK, hw), 1)
    head_masks = [(lane >= hh * SSD_HEAD_DIM) & (lane < (hh + 1) * SSD_HEAD_DIM) for hh in range(nh)]

    neg_a = -jnp.exp(alog_ref[...])
    dt = [_softplus(sm_ref[b] + dtb_ref[...]) for b in rng]
    a_cum = _cumsum_rows_batched(tri_bf, [dt[b] * neg_a for b in rng])
    a_cum_t = [a_cum[b].T for b in rng]
    ex = _dot01_r2(jnp.concatenate([a_cum[b] for b in rng] + [dt[b] for b in rng], axis=0), e8)
    ac_e = [ex[b * CHUNK:(b + 1) * CHUNK] for b in rng]
    dt_e = [ex[(nb + b) * CHUNK:(nb + b + 1) * CHUNK] for b in rng]
    xs = [p_ref[b, :, D_BRANCH:2 * D_BRANCH].astype(F32) for b in rng]
    xd = [xs[b] * dt_e[b] for b in rng]
    xd_bf = [xd[b].astype(BF16) for b in rng]
    a_last_e = [ac_e[b][CHUNK - 1:CHUNK, :] for b in rng]
    xw_bf = [(xd[b] * jnp.exp(a_last_e[b] - ac_e[b])).astype(BF16) for b in rng]
    ea_e = [jnp.exp(ac_e[b]) for b in rng]

    ys = [[] for _ in rng]
    for g in range(SSD_GROUPS):
        gs = slice(g * hw, (g + 1) * hw)
        bg = [p_ref[b, :, 2 * D_BRANCH + g * SSD_STATE:2 * D_BRANCH + (g + 1) * SSD_STATE] for b in rng]
        cg = [p_ref[b, :, 2 * D_BRANCH + gw + g * SSD_STATE:2 * D_BRANCH + gw + (g + 1) * SSD_STATE]
              for b in rng]
        cb = [_dot_nt(cg[b], bg[b]) for b in rng]
        y_off = [_dot(cg[b], st_ref[b, g].astype(BF16)) for b in rng]
        m_cat, x_cat = [], []
        for b in rng:
            m_parts, x_parts = [], []
            for hh in range(nh):
                h = g * nh + hh
                seg = a_cum[b][:, h:h + 1] - a_cum_t[b][h:h + 1, :]
                dec = jnp.exp(jnp.where(tri, seg, NEG_BIG))
                m_parts.append((cb[b] * dec).astype(BF16))
                x_parts.append(jnp.where(head_masks[hh], xd_bf[b][:, gs], jnp.zeros((CHUNK, hw), BF16)))
            m_cat.append(jnp.concatenate(m_parts, axis=1))
            x_cat.append(jnp.concatenate(x_parts, axis=0))
        y_dg = [_dot(m_cat[b], x_cat[b]) for b in rng]
        s_loc = [_dot_tn(bg[b], xw_bf[b][:, gs]) for b in rng]
        for b in rng:
            ys[b].append(y_off[b] * ea_e[b][:, gs] + y_dg[b])
            st_ref[b, g] = st_ref[b, g] * jnp.exp(a_last_e[b][:, gs]) + s_loc[b]

    for b in rng:
        gate = p_ref[b, :, 0:D_BRANCH].astype(F32)
        y = jnp.concatenate(ys[b], axis=1) + dsk_ref[...] * xs[b]
        o_ref[b] = (_rms(y * gate) * nw_ref[...]).astype(o_ref.dtype)


def _ssd_call(l, p_ssd, p_sm, dt_bias, a_log, d_skip, norm_w, bsz, s):
    nc = s // CHUNK
    m = bsz * s
    e8 = np.zeros((LANES, D_BRANCH), np.float32)
    for h in range(SSD_HEADS):
        e8[h, h * SSD_HEAD_DIM:(h + 1) * SSD_HEAD_DIM] = 1.0
    row = lambda n: pl.BlockSpec((bsz, CHUNK, n), lambda c: (0, c, 0))
    return pl.pallas_call(
        _ssd_kernel,
        grid=(nc,),
        in_specs=[row(_W_SSD), row(LANES),
                  _layer_spec(l, (1, LANES), 2), _layer_spec(l, (1, LANES), 2),
                  _layer_spec(l, (1, D_BRANCH), 2), _layer_spec(l, (1, D_BRANCH), 2),
                  pl.BlockSpec((LANES, D_BRANCH), lambda c: (0, 0))],
        out_specs=row(D_BRANCH),
        out_shape=jax.ShapeDtypeStruct((bsz, s, D_BRANCH), BF16),
        scratch_shapes=[pltpu.VMEM((bsz, SSD_GROUPS, SSD_STATE, 256), F32)],
        compiler_params=_cp(("arbitrary",)),
        name="ssd_mixer",
    )(p_ssd.reshape(bsz, s, _W_SSD), p_sm.reshape(bsz, s, _W_SM), dt_bias, a_log, d_skip, norm_w,
      jnp.asarray(e8, BF16)).reshape(m, D_BRANCH)


def _mlstm_kernel(p_ref, sm_ref, ib_ref, fb_ref, nw_ref, jm_ref, o_ref, c_ref, m_ref):
    @pl.when(pl.program_id(0) == 0)
    def _():
        c_ref[...] = jnp.zeros(c_ref.shape, F32)
        m_ref[...] = jnp.zeros(m_ref.shape, F32)

    nb = p_ref.shape[0]
    rng = range(nb)
    d = D_BRANCH
    tri = _tri_mask()
    tri_bf = jnp.where(tri, 1.0, 0.0).astype(BF16)

    i_log = [sm_ref[b, :, LANES:2 * LANES] + ib_ref[...] for b in rng]
    f_log = [-_softplus(-(sm_ref[b, :, 2 * LANES:3 * LANES] + fb_ref[...])) for b in rng]
    f_cum = _cumsum_rows_batched(tri_bf, f_log)
    r = [i_log[b] - f_cum[b] for b in rng]
    r_t = [r[b].T for b in rng]
    a_col, inter, em, s_old, w_shift = [], [], [], [], []
    for b in rng:
        f_last = f_cum[b][CHUNK - 1:CHUNK, :]
        m_prev = m_ref[b]
        m_new = jnp.maximum(f_last + m_prev, jnp.max(f_last + r[b], axis=0, keepdims=True))
        s_old.append(jnp.exp(f_last + m_prev - m_new))
        w_shift.append(f_last - m_new)
        m_ref[b] = m_new
        g_inter = f_cum[b] + m_prev
        m_t = jnp.maximum(g_inter, f_cum[b] + _cummax_rows(r[b]))
        a_col.append(f_cum[b] - m_t)
        inter.append(jnp.exp(g_inter - m_t))
        em.append(jnp.exp(-m_t))

    ones = jnp.ones((CHUNK, ML_HEAD_DIM), BF16)
    ys = [[] for _ in rng]
    for h in range(ML_HEADS):
        c0 = h * ML_HEAD_DIM
        qh = [p_ref[b, :, d + c0:d + c0 + ML_HEAD_DIM] for b in rng]
        kh = [p_ref[b, :, 2 * d + c0:2 * d + c0 + ML_HEAD_DIM] for b in rng]
        v_ext = [jnp.concatenate([p_ref[b, :, 3 * d + c0:3 * d + c0 + ML_HEAD_DIM], ones], axis=1)
                 for b in rng]
        qk = [_dot_nt(qh[b], kh[b]) for b in rng]
        qc = [_dot(qh[b], c_ref[b, h].astype(BF16)) for b in rng]
        sm = []
        for b in rng:
            dmat = jnp.where(tri, a_col[b][:, h:h + 1] + r_t[b][h:h + 1, :], NEG_BIG)
            sm.append((qk[b] * jnp.exp(dmat)).astype(BF16))
        numden = [_dot(sm[b], v_ext[b]) + inter[b][:, h:h + 1] * qc[b] for b in rng]
        kw_t = [kh[b].T * jnp.exp(r_t[b][h:h + 1, :] + w_shift[b][:, h:h + 1]).astype(BF16)
                for b in rng]
        upd = [_dot(kw_t[b], v_ext[b]) for b in rng]
        for b in rng:
            num = numden[b][:, 0:ML_HEAD_DIM]
            den = numden[b][:, ML_HEAD_DIM:]
            o_gate = p_ref[b, :, 4 * d + c0:4 * d + c0 + ML_HEAD_DIM].astype(F32)
            ys[b].append(o_gate * (num / jnp.maximum(jnp.abs(den), em[b][:, h:h + 1])))
            c_ref[b, h] = s_old[b][:, h:h + 1] * c_ref[b, h] + upd[b]

    y = [jnp.concatenate(ys[b], axis=1) for b in rng]
    ms = [_dot((y[b] * y[b]).astype(BF16), jm_ref[...]) for b in rng]
    for b in rng:
        gate = p_ref[b, :, 0:d].astype(F32)
        o_ref[b] = (y[b] * lax.rsqrt(ms[b] + NORM_EPS) * nw_ref[...] * gate).astype(o_ref.dtype)


def _mlstm_call(l, p_ml, p_sm, i_bias, f_bias, norm_w, bsz, s):
    nc = s // CHUNK
    m = bsz * s
    row = lambda n: pl.BlockSpec((bsz, CHUNK, n), lambda c: (0, c, 0))
    jm = np.kron(np.eye(ML_HEADS), np.full((ML_HEAD_DIM, ML_HEAD_DIM), 1.0 / ML_HEAD_DIM))
    return pl.pallas_call(
        _mlstm_kernel,
        grid=(nc,),
        in_specs=[row(_W_ML), row(_W_SM),
                  _layer_spec(l, (1, LANES), 2), _layer_spec(l, (1, LANES), 2),
                  _layer_spec(l, (1, D_BRANCH), 2),
                  pl.BlockSpec((D_BRANCH, D_BRANCH), lambda c: (0, 0))],
        out_specs=row(D_BRANCH),
        out_shape=jax.ShapeDtypeStruct((bsz, s, D_BRANCH), BF16),
        scratch_shapes=[pltpu.VMEM((bsz, ML_HEADS, ML_HEAD_DIM, 2 * ML_HEAD_DIM), F32),
                        pltpu.VMEM((bsz, 1, LANES), F32)],
        compiler_params=_cp(("arbitrary",)),
        name="mlstm_mixer",
    )(p_ml.reshape(bsz, s, _W_ML), p_sm.reshape(bsz, s, _W_SM), i_bias, f_bias,
      norm_w, jnp.asarray(jm, BF16)).reshape(m, D_BRANCH)


def _ret_log_gamma():
    return [math.log1p(-2.0 ** (-(RET_DECAY_BASE + h))) for h in range(RET_HEADS)]


def _ret_consts():
    lg = _ret_log_gamma()
    pos = np.arange(CHUNK, dtype=np.float64)
    rel = pos[:, None] - pos[None, :]
    dm = np.stack([np.where(rel >= 0, np.exp(np.maximum(rel, 0.0) * g), 0.0) for g in lg])
    from_start = np.concatenate(
        [np.repeat(np.exp((pos + 1.0) * g)[:, None], RET_V, axis=1) for g in lg], axis=1)
    to_end = np.concatenate(
        [np.repeat(np.exp((CHUNK - 1.0 - pos) * g)[:, None], RET_QK, axis=1) for g in lg], axis=1)
    qw = RET_HEADS * RET_QK
    swap = np.zeros((qw, qw))
    for j in range(qw):
        base, off = (j // RET_QK) * RET_QK, j % RET_QK
        swap[base + (off + RET_QK // 2) % RET_QK, j] = 1.0
    f = lambda a: jnp.asarray(a, F32)
    return f(dm), f(from_start), f(to_end), jnp.asarray(swap, BF16)


def _ret_kernel(p_ref, cos_ref, sin_ref, dm_ref, fs_ref, te_ref, sw_ref, nw_ref, o_ref, r_ref):
    @pl.when(pl.program_id(0) == 0)
    def _():
        r_ref[...] = jnp.zeros(r_ref.shape, F32)

    nb = p_ref.shape[0]
    d = D_BRANCH
    qw = RET_HEADS * RET_QK
    sw = sw_ref[...]
    lane = lax.broadcasted_iota(jnp.int32, (CHUNK, qw), 1)
    head_masks = [(lane >= h * RET_QK) & (lane < (h + 1) * RET_QK) for h in range(RET_HEADS)]

    def v_of(b):
        return p_ref[b, :, d + 2 * qw:2 * d + 2 * qw]

    q_raw = [p_ref[b, :, d:d + qw] for b in range(nb)]
    k_raw = [p_ref[b, :, d + qw:d + 2 * qw] for b in range(nb)]
    q_sw = [_dot(q_raw[b], sw) for b in range(nb)]
    k_sw = [_dot(k_raw[b], sw) for b in range(nb)]
    qs, ks = [], []
    for b in range(nb):
        cos_e = jnp.concatenate([cos_ref[b]] * (qw // LANES), axis=1)
        sin_e = jnp.concatenate([sin_ref[b]] * (qw // LANES), axis=1)
        qs.append(q_raw[b].astype(F32) * cos_e + q_sw[b] * sin_e)
        ks.append((k_raw[b].astype(F32) * cos_e + k_sw[b] * sin_e) * (RET_QK ** -0.5))
    k_bf = [k.astype(BF16) for k in ks]
    q_bf = [q.astype(BF16) for q in qs]
    def state_operand(b):
        rows = []
        for h in range(RET_HEADS):
            parts = [jnp.zeros((RET_QK, RET_V), BF16)] * RET_HEADS
            parts[h] = r_ref[b, h].astype(BF16)
            rows.append(jnp.concatenate(parts, axis=1))
        return jnp.concatenate(rows, axis=0)

    ys = [_dot(q_bf[b], state_operand(b)) * fs_ref[...] for b in range(nb)]
    inner = [[] for _ in range(nb)]
    for h in range(RET_HEADS):
        scores = [(_dot_nt(jnp.where(head_masks[h], q_bf[b], jnp.zeros_like(q_bf[b])), k_bf[b])
                   * dm_ref[h]).astype(BF16) for b in range(nb)]
        for b in range(nb):
            inner[b].append(_dot(scores[b], v_of(b)[:, h * RET_V:(h + 1) * RET_V]))
    ys = [ys[b] + jnp.concatenate(inner[b], axis=1) for b in range(nb)]
    upd = [_dot_tn((ks[b] * te_ref[...]).astype(BF16), v_of(b)) for b in range(nb)]
    chunk_decay = [math.exp(CHUNK * g) for g in _ret_log_gamma()]
    for b in range(nb):
        for h in range(RET_HEADS):
            blk = upd[b][h * RET_QK:(h + 1) * RET_QK, h * RET_V:(h + 1) * RET_V]
            r_ref[b, h] = r_ref[b, h] * chunk_decay[h] + blk
    for b in range(nb):
        gate = p_ref[b, :, 0:d].astype(F32)
        outs = [_rms(ys[b][:, h * RET_V:(h + 1) * RET_V]) for h in range(RET_HEADS)]
        o_ref[b] = (jnp.concatenate(outs, axis=1) * nw_ref[...] * gate).astype(o_ref.dtype)


def _ret_call(l, p_ret, cos_t, sin_t, consts, norm_w, bsz, s):
    nc = s // CHUNK
    m = bsz * s
    qw, vw = RET_HEADS * RET_QK, RET_HEADS * RET_V
    dm, fs, te, sw = consts
    const = lambda shape: pl.BlockSpec(shape, lambda c: (0,) * len(shape))
    row = lambda n: pl.BlockSpec((bsz, CHUNK, n), lambda c: (0, c, 0))
    return pl.pallas_call(
        _ret_kernel,
        grid=(nc,),
        in_specs=[row(_W_RET), row(LANES), row(LANES),
                  const((RET_HEADS, CHUNK, CHUNK)), const((CHUNK, vw)), const((CHUNK, qw)),
                  const((qw, qw)), _layer_spec(l, (1, D_BRANCH), 2)],
        out_specs=row(D_BRANCH),
        out_shape=jax.ShapeDtypeStruct((bsz, s, D_BRANCH), BF16),
        scratch_shapes=[pltpu.VMEM((bsz, RET_HEADS, RET_QK, RET_V), F32)],
        compiler_params=_cp(("arbitrary",)),
        name="retention_mixer",
    )(p_ret.reshape(bsz, s, _W_RET), cos_t.reshape(bsz, s, LANES), sin_t.reshape(bsz, s, LANES),
      dm, fs, te, sw, norm_w).reshape(m, D_BRANCH)


def _s5_compact(lam_re, lam_im, b_re, b_im, c_re, c_im, log_step):
    depth = lam_re.shape[0]
    nb, ngl, sub = S5_LANE_BLOCKS, S5_LOCAL_GROUPS, S5_SUB
    step = jnp.exp(log_step.astype(F32))[..., None]
    lr = jnp.minimum(lam_re.astype(F32), -1e-4)
    li = lam_im.astype(F32)
    mag = jnp.exp(lr * step)
    ang = li * step
    ab_re = mag * jnp.cos(ang)
    ab_im = mag * jnp.sin(ang)
    den = lr * lr + li * li
    coef_re = ((ab_re - 1.0) * lr + ab_im * li) / den
    coef_im = (ab_im * lr - (ab_re - 1.0) * li) / den
    br_t, bi_t = jnp.swapaxes(b_re.astype(F32), -1, -2), jnp.swapaxes(b_im.astype(F32), -1, -2)
    cf_re, cf_im = coef_re[:, :, None, :], coef_im[:, :, None, :]
    bb_re = cf_re * br_t - cf_im * bi_t
    bb_im = cf_re * bi_t + cf_im * br_t
    pw_re, pw_im = [jnp.ones_like(ab_re)], [jnp.zeros_like(ab_im)]
    for _ in range(sub):
        pr, pi = pw_re[-1], pw_im[-1]
        pw_re.append(pr * ab_re - pi * ab_im)
        pw_im.append(pr * ab_im + pi * ab_re)
    cr, ci = c_re.astype(F32), c_im.astype(F32)
    cr_t, ci_t = jnp.swapaxes(cr, -1, -2), jnp.swapaxes(ci, -1, -2)
    hp = lax.Precision.HIGHEST
    abb_re = [pw_re[k][:, :, None, :] * bb_re - pw_im[k][:, :, None, :] * bb_im for k in range(sub)]
    abb_im = [pw_re[k][:, :, None, :] * bb_im + pw_im[k][:, :, None, :] * bb_re for k in range(sub)]
    taps_t = [jnp.einsum('lgip,lgop->lgio', abb_re[k], cr, precision=hp)
              - jnp.einsum('lgip,lgop->lgio', abb_im[k], ci, precision=hp) for k in range(sub)]

    def by_block(rows):
        a = jnp.stack(rows).reshape(len(rows), depth, nb, ngl, rows[0].shape[-2], rows[0].shape[-1])
        return a.transpose(1, 2, 0, 3, 4, 5).reshape(depth, nb, -1, rows[0].shape[-1])

    c_in = jnp.concatenate([by_block([abb_re[sub - 1 - t] for t in range(sub)]),
                            by_block([abb_im[sub - 1 - t] for t in range(sub)])], axis=-1)
    zero = jnp.zeros_like(taps_t[0])
    toe = by_block([jnp.concatenate([zero] * t + taps_t[:sub - t], axis=-1) for t in range(sub)])

    def rows_x(c_t, s_t, sign):
        cols = [c_t * pw_re[k][..., None] - sign * s_t * pw_im[k][..., None] for k in range(1, sub + 1)]
        return jnp.stack(cols, axis=-2).reshape(depth, nb, ngl * S5_STATE, sub * S5_GROUP)

    x_re = rows_x(cr_t, ci_t, 1.0)
    x_im = -(jnp.stack([cr_t * pw_im[k][..., None] + ci_t * pw_re[k][..., None]
                        for k in range(1, sub + 1)], axis=-2)
             .reshape(depth, nb, ngl * S5_STATE, sub * S5_GROUP))
    c_out = jnp.concatenate([toe, x_re, x_im], axis=2)
    a8 = jnp.concatenate([pw_re[sub].reshape(depth, nb, 1, ngl * S5_STATE),
                          pw_im[sub].reshape(depth, nb, 1, ngl * S5_STATE)], axis=-1)
    return c_in.astype(BF16), c_out.astype(BF16), a8


def _s5_expanders():
    kin = S5_SUB * LANES
    half = S5_LOCAL_GROUPS * S5_STATE
    sel_in = np.zeros((LANES, 2 * half), np.float32)
    csel_in = np.zeros((LANES, 2 * half), np.float32)
    for col in range(2 * half):
        sel_in[(col // half) * S5_STATE + col % S5_STATE, col] = 1.0
        csel_in[(col % half) // S5_STATE, col] = 1.0
    sel_out = np.zeros((LANES, kin), np.float32)
    csel_out = np.zeros((LANES, kin), np.float32)
    for col in range(kin):
        sel_out[(col // LANES) * S5_GROUP + col % S5_GROUP, col] = 1.0
        csel_out[(col // S5_GROUP) % S5_LOCAL_GROUPS, col] = 1.0
    rsel_in = np.zeros((kin, LANES), np.float32)
    for r in range(kin):
        rsel_in[r, (r // S5_GROUP) % S5_LOCAL_GROUPS] = 1.0
    rsel_out = np.zeros((2 * kin, LANES), np.float32)
    rsel_out[:kin] = rsel_in
    for r in range(kin):
        rsel_out[kin + r, (r % half) // S5_STATE] = 1.0
    return tuple(jnp.asarray(a, BF16) for a in (sel_in, csel_in, rsel_in, sel_out, csel_out, rsel_out))


def _s5_fold(a, bsz, s):
    j = s // S5_SUB
    a = a.reshape(bsz, j, S5_SUB, S5_LANE_BLOCKS, LANES).transpose(1, 0, 3, 2, 4)
    return a.reshape(j * bsz, S5_LANE_BLOCKS * S5_SUB * LANES)


def _s5_unfold(a, bsz, s):
    j = s // S5_SUB
    a = a.reshape(j, bsz, S5_LANE_BLOCKS, S5_SUB, LANES).transpose(1, 0, 3, 2, 4)
    return a.reshape(bsz * s, D_BRANCH)


S5_ROW_BLOCK = 512


def _s5_scan_kernel(u_ref, cin_ref, cout_ref, a8_ref, sin_ref, csin_ref, rsin_ref,
                    sout_ref, csout_ref, rsout_ref, o_ref, x_ref, min_ref, mout_ref, *, bsz):
    rows = u_ref.shape[0]
    half = S5_LOCAL_GROUPS * S5_STATE
    rb = min(S5_ROW_BLOCK, rows)

    min_ref[...] = (_dot(cin_ref[...], sin_ref[...]) * _dot(rsin_ref[...], csin_ref[...])).astype(BF16)
    mout_ref[...] = (_dot(cout_ref[...], sout_ref[...]) * _dot(rsout_ref[...], csout_ref[...])).astype(BF16)

    for r0 in range(0, rows, rb):
        x_ref[r0:r0 + rb, :] = _dot(u_ref[r0:r0 + rb, :], min_ref[...])

    a_re = a8_ref[:, 0:half]
    a_im = a8_ref[:, half:]

    def body(i, carry):
        x_re, x_im = carry
        r0 = pl.multiple_of(i * bsz, bsz)
        loc = x_ref[pl.ds(r0, bsz), :]
        x_ref[pl.ds(r0, bsz), :] = jnp.concatenate([x_re, x_im], axis=1)
        n_re = a_re * x_re - a_im * x_im + loc[:, 0:half]
        n_im = a_re * x_im + a_im * x_re + loc[:, half:]
        return n_re, n_im

    zero = jnp.zeros((bsz, half), F32)
    lax.fori_loop(0, rows // bsz, body, (zero, zero), unroll=8)

    kin = u_ref.shape[1]
    nt_w = 2 * LANES
    for r0 in range(0, rows, rb):
        xb = x_ref[r0:r0 + rb, :].astype(BF16)
        for c0 in range(0, kin, nt_w):
            k_u = c0 + nt_w
            y = _dot(u_ref[r0:r0 + rb, 0:k_u], mout_ref[0:k_u, c0:c0 + nt_w])
            y = y + _dot(xb, mout_ref[kin:2 * kin, c0:c0 + nt_w])
            o_ref[r0:r0 + rb, c0:c0 + nt_w] = y.astype(o_ref.dtype)


def _s5_scan_call(l, u_fold, c_in, c_out, a8, expanders, bsz):
    rows = u_fold.shape[0]
    kin = S5_SUB * LANES
    lspec = lambda shape: pl.BlockSpec((None, None) + shape, lambda lb: (l, lb, 0, 0))
    const = lambda a: pl.BlockSpec(a.shape, lambda lb: (0, 0))
    sel_in, csel_in, rsel_in, sel_out, csel_out, rsel_out = expanders
    return pl.pallas_call(
        functools.partial(_s5_scan_kernel, bsz=bsz),
        grid=(S5_LANE_BLOCKS,),
        in_specs=[pl.BlockSpec((rows, kin), lambda lb: (0, lb)),
                  lspec((kin, LANES)), lspec((2 * kin, LANES)), lspec((1, kin)),
                  const(sel_in), const(csel_in), const(rsel_in),
                  const(sel_out), const(csel_out), const(rsel_out)],
        out_specs=pl.BlockSpec((rows, kin), lambda lb: (0, lb)),
        out_shape=jax.ShapeDtypeStruct(u_fold.shape, BF16),
        scratch_shapes=[pltpu.VMEM((rows, kin), F32),
                        pltpu.VMEM((kin, kin), BF16), pltpu.VMEM((2 * kin, kin), BF16)],
        compiler_params=_cp(("parallel",)),
        name="s5_scan",
    )(u_fold, c_in, c_out, a8, sel_in, csel_in, rsel_in, sel_out, csel_out, rsel_out)


def _outproj_kernel(y_ssd, y_ml, s5g_ref, s5u_ref, s5y_ref, y_ret, w_ref, h_ref, g_ref, fw_ref,
                    d_ref, wg_ref, bg_ref, nw5_ref, o_ref, *, final_norm):
    d = D_BRANCH
    u = s5u_ref[...].astype(F32)
    y = s5y_ref[...].astype(F32) + d_ref[...] * u
    y = 0.5 * y * (1.0 + jnp.tanh(math.sqrt(2.0 / math.pi) * (y + 0.044715 * (y * y * y))))
    g = _dot(y.astype(BF16), wg_ref[...]) + bg_ref[...]
    y = g[:, 0:d] * _sigmoid(g[:, d:2 * d])
    y_s5 = (_rms(y) * nw5_ref[...] * s5g_ref[...].astype(F32)).astype(BF16)

    acc = _dot(y_ssd[...], w_ref[0 * d:1 * d, :])
    acc = acc + _dot(y_ml[...], w_ref[1 * d:2 * d, :])
    acc = acc + _dot(y_s5, w_ref[2 * d:3 * d, :])
    acc = acc + _dot(y_ret[...], w_ref[3 * d:4 * d, :])
    h = h_ref[...] + g_ref[...] * acc
    if final_norm:
        h = _rms(h) * fw_ref[...]
    o_ref[...] = h


def _outproj_call(l, y_ssd, y_ml, p_s5g, p_s5u, y_ssm, y_ret, w_out, h, mod5, final_w,
                  s5_d, w_glu, b_glu, s5_nw, s, final_norm):
    m = h.shape[0]
    tm = min(ROW_TILE, s)
    per_batch = s // tm
    yspec = pl.BlockSpec((tm, D_BRANCH), lambda i: (i, 0))
    return pl.pallas_call(
        functools.partial(_outproj_kernel, final_norm=final_norm),
        grid=(m // tm,),
        in_specs=[yspec] * 6 + [
            _layer_spec(l, (4 * D_BRANCH, D_MODEL), 1),
            pl.BlockSpec((tm, D_MODEL), lambda i: (i, 0)),
            pl.BlockSpec((None, None, None, 1, D_MODEL), lambda i: (l, i // per_batch, 2, 0, 0)),
            pl.BlockSpec((1, D_MODEL), lambda i: (0, 0)),
            _layer_spec(l, (1, D_BRANCH), 1), _layer_spec(l, (D_BRANCH, 2 * D_BRANCH), 1),
            _layer_spec(l, (1, 2 * D_BRANCH), 1), _layer_spec(l, (1, D_BRANCH), 1),
        ],
        out_specs=pl.BlockSpec((tm, D_MODEL), lambda i: (i, 0)),
        out_shape=jax.ShapeDtypeStruct((m, D_MODEL), F32),
        compiler_params=_cp(("parallel",)),
        name="out_proj",
    )(y_ssd, y_ml, p_s5g, p_s5u, y_ssm, y_ret, w_out, h, mod5, final_w, s5_d, w_glu, b_glu, s5_nw)


def _pack_w_in(w_in):
    assert w_in.shape[-1] == _D_IN
    pad = lambda a: jnp.pad(a, [(0, 0)] * (a.ndim - 1) + [(0, LANES - a.shape[-1])])
    parts = [w_in[..., _O_SSD_Z:_O_SSD_DT], w_in[..., _O_ML_Z:_O_ML_I], w_in[..., _O_S5_Z:_O_RET_Z],
             w_in[..., _O_RET_Z:_D_IN], pad(w_in[..., _O_SSD_DT:_O_ML_Z]),
             pad(w_in[..., _O_ML_I:_O_ML_F]), pad(w_in[..., _O_ML_F:_O_S5_Z])]
    return jnp.concatenate(parts, axis=-1).astype(BF16)


def _pad_lanes(v):
    return jnp.pad(v.astype(F32), ((0, 0), (0, LANES - v.shape[-1])))[:, None, :]


def kernel(x, c, positions, norm_w, w_ada, b_ada, w_in, w_out, ssd_conv_w, ssd_conv_b, ssd_dt_bias, ssd_a_log, ssd_d, ssd_norm_w, ml_conv_w, ml_conv_b, ml_i_bias, ml_f_bias, ml_norm_w, s5_lambda_re, s5_lambda_im, s5_b_re, s5_b_im, s5_c_re, s5_c_im, s5_d, s5_log_step, s5_w_glu, s5_b_glu, s5_norm_w, ret_norm_w, final_norm_w):
    bsz, s, d = x.shape
    depth = w_in.shape[0]
    assert d == D_MODEL and s % CHUNK == 0 and bsz % 8 == 0
    m = bsz * s
    row3 = lambda a: a.astype(F32)[:, None, :]

    mod = _mod_call(c.astype(F32), w_ada, b_ada)
    mod5 = mod.reshape(depth, bsz, 3, 1, D_MODEL)
    cos_t, sin_t = _rope_call(positions)
    ret_consts = _ret_consts()
    w_all = _pack_w_in(w_in)
    w_out_bf = w_out.astype(BF16)
    w_glu_bf = s5_w_glu.astype(BF16)
    c_in, c_out, a8 = _s5_compact(s5_lambda_re, s5_lambda_im, s5_b_re, s5_b_im, s5_c_re, s5_c_im,
                                  s5_log_step)
    s5_exp = _s5_expanders()
    norm_w3, final_w = row3(norm_w), final_norm_w.reshape(1, D_MODEL).astype(F32)
    ssd_cb3, ml_cb3 = row3(ssd_conv_b), row3(ml_conv_b)
    dt_bias3, a_log3 = _pad_lanes(ssd_dt_bias), _pad_lanes(ssd_a_log)
    d_skip3 = row3(jnp.repeat(ssd_d, SSD_HEAD_DIM, axis=-1))
    i_bias3, f_bias3 = _pad_lanes(ml_i_bias), _pad_lanes(ml_f_bias)
    ssd_nw3, ml_nw3, s5_nw3, ret_nw3 = row3(ssd_norm_w), row3(ml_norm_w), row3(s5_norm_w), row3(ret_norm_w)
    s5_d3, b_glu3 = row3(s5_d), row3(s5_b_glu)

    h = x.reshape(m, D_MODEL)
    for l in range(depth):
        p_ssd, p_ml, p_s5g, p_s5u, p_ret, p_sm = _inproj_call(
            l, h, norm_w3, mod5, w_all, ssd_conv_w, ssd_cb3, ml_conv_w, ml_cb3, s)
        y_ssd = _ssd_call(l, p_ssd, p_sm, dt_bias3, a_log3, d_skip3, ssd_nw3, bsz, s)
        y_ml = _mlstm_call(l, p_ml, p_sm, i_bias3, f_bias3, ml_nw3, bsz, s)
        y_fold = _s5_scan_call(l, _s5_fold(p_s5u, bsz, s), c_in, c_out, a8, s5_exp, bsz)
        y_ssm = _s5_unfold(y_fold, bsz, s)
        y_ret = _ret_call(l, p_ret, cos_t, sin_t, ret_consts, ret_nw3, bsz, s)
        h = _outproj_call(l, y_ssd, y_ml, p_s5g, p_s5u, y_ssm, y_ret, w_out_bf, h, mod5, final_w,
                          s5_d3, w_glu_bf, b_glu3, s5_nw3, s, final_norm=(l == depth - 1))
    return h.reshape(bsz, s, D_MODEL)
```

```python
import functools
import math

import numpy as np
import jax
import jax.numpy as jnp
from jax import lax
from jax.experimental import pallas as pl
from jax.experimental.pallas import tpu as pltpu

F32 = jnp.float32
BF16 = jnp.bfloat16

D_MODEL = 1024
D_BRANCH = 512
CHUNK = 128
CONV_K = 4
CONV_PAD = 8
NORM_EPS = 1e-6
NEG_BIG = -1e30

SSD_HEADS = 8
SSD_HEAD_DIM = 64
SSD_GROUPS = 2
SSD_STATE = 128
ML_HEADS = 4
ML_HEAD_DIM = 128
S5_GROUPS = 32
S5_GROUP = 16
S5_STATE = 64
S5_SUB = 8
S5_LANE_BLOCKS = 4
S5_LOCAL_GROUPS = S5_GROUPS // S5_LANE_BLOCKS
RET_HEADS = 4
RET_QK = 64
RET_V = 128
RET_DECAY_BASE = 5.0
ROPE_BASE = 10000.0
LANES = 128

_O_SSD_Z, _O_SSD_DT = 0, 1536
_O_ML_Z, _O_ML_I, _O_ML_F = 1544, 4104, 4108
_O_S5_Z = 4112
_O_RET_Z = 5136
_D_IN = 6672
_W_SSD, _W_ML, _W_S5, _W_RET, _W_SM = 1536, 2560, 1024, 1536, 384
_W_ALL = _W_SSD + _W_ML + _W_S5 + _W_RET + _W_SM
_W_S5G = _W_S5 - D_BRANCH

VMEM_LIMIT = 56 * 1024 * 1024
ROW_TILE = 512


def _cp(sem):
    return pltpu.CompilerParams(dimension_semantics=sem, vmem_limit_bytes=VMEM_LIMIT)


def _layer_spec(l, shape, nargs):
    return pl.BlockSpec((None,) + tuple(shape), lambda *_: (l,) + (0,) * len(shape))


def _dot(a, b):
    return jnp.dot(a, b, preferred_element_type=F32)


def _dot_nt(a, b):
    return lax.dot_general(a, b, (((1,), (1,)), ((), ())), preferred_element_type=F32)


def _dot_tn(a, b):
    return lax.dot_general(a, b, (((0,), (0,)), ((), ())), preferred_element_type=F32)


def _sigmoid(x):
    return 0.5 + 0.5 * jnp.tanh(0.5 * x)


def _silu(x):
    hx = 0.5 * x
    return hx + hx * jnp.tanh(hx)


def _softplus(x):
    return jnp.maximum(x, 0.0) + jnp.log(1.0 + jnp.exp(-jnp.abs(x)))


def _split3(v):
    hi = v.astype(BF16)
    r1 = v - hi.astype(F32)
    mid = r1.astype(BF16)
    lo = (r1 - mid.astype(F32)).astype(BF16)
    return hi, mid, lo


def _dot01_r2(v, m01):
    hi = v.astype(BF16)
    lo = (v - hi.astype(F32)).astype(BF16)
    return _dot(hi, m01) + _dot(lo, m01)


def _cumsum_rows_batched(tri_bf, xs):
    terms = []
    for x in xs:
        terms.extend(_split3(x))
    res = _dot(tri_bf, jnp.concatenate(terms, axis=1))
    w = xs[0].shape[1]
    return [res[:, (3 * i) * w:(3 * i + 1) * w] + res[:, (3 * i + 1) * w:(3 * i + 2) * w]
            + res[:, (3 * i + 2) * w:(3 * i + 3) * w] for i in range(len(xs))]


def _cummax_rows(x):
    n = x.shape[0]
    row = lax.broadcasted_iota(jnp.int32, x.shape, 0)
    sh = 1
    while sh < n:
        x = jnp.maximum(x, jnp.where(row >= sh, pltpu.roll(x, sh, axis=0), NEG_BIG))
        sh *= 2
    return x


def _tri_mask():
    row = lax.broadcasted_iota(jnp.int32, (CHUNK, CHUNK), 0)
    col = lax.broadcasted_iota(jnp.int32, (CHUNK, CHUNK), 1)
    return row >= col


def _rms(y):
    return y * lax.rsqrt(jnp.mean(y * y, axis=-1, keepdims=True) + NORM_EPS)


def _mod_kernel(c_ref, w_ref, b_ref, o_ref):
    cond = _silu(c_ref[...])
    o_ref[0] = _dot(cond.astype(BF16), w_ref[0].astype(BF16)) + b_ref[0]


def _mod_call(c, w_ada, b_ada):
    depth, d, n = w_ada.shape
    bsz = c.shape[0]
    tn = 1024
    return pl.pallas_call(
        _mod_kernel,
        grid=(depth, n // tn),
        in_specs=[
            pl.BlockSpec((bsz, d), lambda l, j: (0, 0)),
            pl.BlockSpec((1, d, tn), lambda l, j: (l, 0, j)),
            pl.BlockSpec((1, 1, tn), lambda l, j: (l, 0, j)),
        ],
        out_specs=pl.BlockSpec((1, bsz, tn), lambda l, j: (l, 0, j)),
        out_shape=jax.ShapeDtypeStruct((depth, bsz, n), F32),
        compiler_params=_cp(("parallel", "parallel")),
        name="adaln_mod",
    )(c, w_ada, b_ada.reshape(depth, 1, n))


def _rope_kernel(pos_ref, f_ref, sg_ref, cos_ref, sin_ref):
    ang = pos_ref[...] * f_ref[...]
    cos_ref[...] = jnp.cos(ang)
    sin_ref[...] = jnp.sin(ang) * sg_ref[...]


def _rope_call(positions):
    bsz, s = positions.shape
    m = bsz * s
    half = RET_QK // 2
    inv_freq = jnp.exp(-math.log(ROPE_BASE) * jnp.arange(half, dtype=F32) / half)
    f128 = jnp.tile(inv_freq, LANES // half).reshape(1, LANES)
    sign = np.tile(np.concatenate([-np.ones(half), np.ones(half)]), LANES // RET_QK)
    sign = jnp.asarray(sign.reshape(1, LANES), F32)
    pos = jnp.broadcast_to(positions.astype(F32).reshape(m, 1), (m, LANES))
    tm = min(1024, m)
    return pl.pallas_call(
        _rope_kernel,
        grid=(m // tm,),
        in_specs=[
            pl.BlockSpec((tm, LANES), lambda i: (i, 0)),
            pl.BlockSpec((1, LANES), lambda i: (0, 0)),
            pl.BlockSpec((1, LANES), lambda i: (0, 0)),
        ],
        out_specs=[pl.BlockSpec((tm, LANES), lambda i: (i, 0))] * 2,
        out_shape=[jax.ShapeDtypeStruct((m, LANES), F32)] * 2,
        compiler_params=_cp(("parallel",)),
        name="rope_tables",
    )(pos, f128, sign)


def _conv_silu(x, buf_ref, cw_ref, cb_ref):
    n = x.shape[0]
    buf_ref[CONV_PAD:CONV_PAD + n, :] = x
    acc = cb_ref[...] + cw_ref[CONV_K - 1:CONV_K, :] * x
    for k in range(CONV_K - 1):
        off = CONV_PAD - (CONV_K - 1) + k
        acc = acc + cw_ref[k:k + 1, :] * buf_ref[off:off + n, :]
    buf_ref[0:CONV_PAD, :] = x[n - CONV_PAD:n, :]
    return _silu(acc)


def _inproj_kernel(h_ref, nw_ref, sc_ref, sh_ref, w_ref, scw_ref, scb_ref, mcw_ref, mcb_ref,
                   o_ssd, o_ml, o_s5g, o_s5u, o_ret, o_sm, sbuf_ref, mbuf_ref, *, per_batch):
    @pl.when(pl.program_id(0) % per_batch == 0)
    def _():
        sbuf_ref[0:CONV_PAD, :] = jnp.zeros((CONV_PAD, sbuf_ref.shape[1]), F32)
        mbuf_ref[0:CONV_PAD, :] = jnp.zeros((CONV_PAD, mbuf_ref.shape[1]), F32)

    x = h_ref[...]
    hn = _rms(x) * nw_ref[...]
    hb = (hn * (1.0 + sc_ref[...]) + sh_ref[...]).astype(BF16)
    d = D_BRANCH

    def proj(col, width):
        return _dot(hb, w_ref[:, col:col + width])

    col = 0
    o_ssd[:, 0:d] = _silu(proj(col, d)).astype(BF16)
    o_ssd[:, d:3 * d] = _conv_silu(proj(col + d, 2 * d), sbuf_ref, scw_ref, scb_ref).astype(BF16)
    col += _W_SSD
    o_ml[:, 0:d] = _silu(proj(col, d)).astype(BF16)
    qk = _conv_silu(proj(col + d, 2 * d), mbuf_ref, mcw_ref, mcb_ref)
    o_ml[:, d:2 * d] = qk[:, 0:d].astype(BF16)
    o_ml[:, 2 * d:3 * d] = (qk[:, d:2 * d] * (ML_HEAD_DIM ** -0.5)).astype(BF16)
    o_ml[:, 3 * d:4 * d] = proj(col + 3 * d, d).astype(BF16)
    o_ml[:, 4 * d:5 * d] = _sigmoid(proj(col + 4 * d, d)).astype(BF16)
    col += _W_ML
    o_s5g[...] = _silu(proj(col, d)).astype(BF16)
    o_s5u[...] = proj(col + d, d).astype(BF16)
    col += _W_S5
    o_ret[:, 0:d] = _silu(proj(col, d)).astype(BF16)
    o_ret[:, d:3 * d] = proj(col + d, 2 * d).astype(BF16)
    col += _W_RET
    o_sm[...] = proj(col, _W_SM)


def _inproj_call(l, h, norm_w, mod5, w_all, ssd_cw, ssd_cb, ml_cw, ml_cb, s):
    m = h.shape[0]
    tm = min(ROW_TILE, s)
    per_batch = s // tm
    widths = (_W_SSD, _W_ML, _W_S5G, D_BRANCH, _W_RET, _W_SM)
    dtypes = (BF16, BF16, BF16, BF16, BF16, F32)
    cdim = 2 * D_BRANCH
    mod_spec = lambda which: pl.BlockSpec((None, None, None, 1, D_MODEL),
                                          lambda i: (l, i // per_batch, which, 0, 0))
    return pl.pallas_call(
        functools.partial(_inproj_kernel, per_batch=per_batch),
        grid=(m // tm,),
        in_specs=[
            pl.BlockSpec((tm, D_MODEL), lambda i: (i, 0)),
            _layer_spec(l, (1, D_MODEL), 1),
            mod_spec(1), mod_spec(0),
            pl.BlockSpec((None, D_MODEL, _W_ALL), lambda i: (l, 0, 0), pipeline_mode=pl.Buffered(1)),
            _layer_spec(l, (CONV_K, cdim), 1), _layer_spec(l, (1, cdim), 1),
            _layer_spec(l, (CONV_K, cdim), 1), _layer_spec(l, (1, cdim), 1),
        ],
        out_specs=[pl.BlockSpec((tm, n), lambda i: (i, 0)) for n in widths],
        out_shape=[jax.ShapeDtypeStruct((m, n), dt) for n, dt in zip(widths, dtypes)],
        scratch_shapes=[pltpu.VMEM((tm + CONV_PAD, cdim), F32), pltpu.VMEM((tm + CONV_PAD, cdim), F32)],
        compiler_params=_cp(("arbitrary",)),
        name="in_proj",
    )(h, norm_w, mod5, mod5, w_all, ssd_cw, ssd_cb, ml_cw, ml_cb)


def _ssd_kernel(p_ref, sm_ref, dtb_ref, alog_ref, dsk_ref, nw_ref, e8_ref, o_ref, st_ref):
    @pl.when(pl.program_id(0) == 0)
    def _():
        st_ref[...] = jnp.zeros(st_ref.shape, F32)

    nb = p_ref.shape[0]
    rng = range(nb)
    tri = _tri_mask()
    tri_bf = jnp.where(tri, 1.0, 0.0).astype(BF16)
    e8 = e8_ref[...]
    gw = SSD_GROUPS * SSD_STATE
    nh = SSD_HEADS // SSD_GROUPS
    hw = nh * SSD_HEAD_DIM
    lane = lax.broadcasted_iota(jnp.int32, (CHUNK, hw), 1)
    head_masks = [(lane >= hh * SSD_HEAD_DIM) & (lane < (hh + 1) * SSD_HEAD_DIM) for hh in range(nh)]

    neg_a = -jnp.exp(alog_ref[...])
    dt = [_softplus(sm_ref[b] + dtb_ref[...]) for b in rng]
    a_cum = _cumsum_rows_batched(tri_bf, [dt[b] * neg_a for b in rng])
    a_cum_t = [a_cum[b].T for b in rng]
    ex = _dot01_r2(jnp.concatenate([a_cum[b] for b in rng] + [dt[b] for b in rng], axis=0), e8)
    ac_e = [ex[b * CHUNK:(b + 1) * CHUNK] for b in rng]
    dt_e = [ex[(nb + b) * CHUNK:(nb + b + 1) * CHUNK] for b in rng]
    xs = [p_ref[b, :, D_BRANCH:2 * D_BRANCH].astype(F32) for b in rng]
    xd = [xs[b] * dt_e[b] for b in rng]
    xd_bf = [xd[b].astype(BF16) for b in rng]
    a_last_e = [ac_e[b][CHUNK - 1:CHUNK, :] for b in rng]
    xw_bf = [(xd[b] * jnp.exp(a_last_e[b] - ac_e[b])).astype(BF16) for b in rng]
    ea_e = [jnp.exp(ac_e[b]) for b in rng]

    ys = [[] for _ in rng]
    for g in range(SSD_GROUPS):
        gs = slice(g * hw, (g + 1) * hw)
        bg = [p_ref[b, :, 2 * D_BRANCH + g * SSD_STATE:2 * D_BRANCH + (g + 1) * SSD_STATE] for b in rng]
        cg = [p_ref[b, :, 2 * D_BRANCH + gw + g * SSD_STATE:2 * D_BRANCH + gw + (g + 1) * SSD_STATE]
              for b in rng]
        cb = [_dot_nt(cg[b], bg[b]) for b in rng]
        y_off = [_dot(cg[b], st_ref[b, g].astype(BF16)) for b in rng]
        m_cat, x_cat = [], []
        for b in rng:
            m_parts, x_parts = [], []
            for hh in range(nh):
                h = g * nh + hh
                seg = a_cum[b][:, h:h + 1] - a_cum_t[b][h:h + 1, :]
                dec = jnp.exp(jnp.where(tri, seg, NEG_BIG))
                m_parts.append((cb[b] * dec).astype(BF16))
                x_parts.append(jnp.where(head_masks[hh], xd_bf[b][:, gs], jnp.zeros((CHUNK, hw), BF16)))
            m_cat.append(jnp.concatenate(m_parts, axis=1))
            x_cat.append(jnp.concatenate(x_parts, axis=0))
        y_dg = [_dot(m_cat[b], x_cat[b]) for b in rng]
        s_loc = [_dot_tn(bg[b], xw_bf[b][:, gs]) for b in rng]
        for b in rng:
            ys[b].append(y_off[b] * ea_e[b][:, gs] + y_dg[b])
            st_ref[b, g] = st_ref[b, g] * jnp.exp(a_last_e[b][:, gs]) + s_loc[b]

    for b in rng:
        gate = p_ref[b, :, 0:D_BRANCH].astype(F32)
        y = jnp.concatenate(ys[b], axis=1) + dsk_ref[...] * xs[b]
        o_ref[b] = (_rms(y * gate) * nw_ref[...]).astype(o_ref.dtype)


def _ssd_call(l, p_ssd, p_sm, dt_bias, a_log, d_skip, norm_w, bsz, s):
    nc = s // CHUNK
    m = bsz * s
    e8 = np.zeros((LANES, D_BRANCH), np.float32)
    for h in range(SSD_HEADS):
        e8[h, h * SSD_HEAD_DIM:(h + 1) * SSD_HEAD_DIM] = 1.0
    row = lambda n: pl.BlockSpec((bsz, CHUNK, n), lambda c: (0, c, 0))
    return pl.pallas_call(
        _ssd_kernel,
        grid=(nc,),
        in_specs=[row(_W_SSD), row(LANES),
                  _layer_spec(l, (1, LANES), 2), _layer_spec(l, (1, LANES), 2),
                  _layer_spec(l, (1, D_BRANCH), 2), _layer_spec(l, (1, D_BRANCH), 2),
                  pl.BlockSpec((LANES, D_BRANCH), lambda c: (0, 0))],
        out_specs=row(D_BRANCH),
        out_shape=jax.ShapeDtypeStruct((bsz, s, D_BRANCH), BF16),
        scratch_shapes=[pltpu.VMEM((bsz, SSD_GROUPS, SSD_STATE, 256), F32)],
        compiler_params=_cp(("arbitrary",)),
        name="ssd_mixer",
    )(p_ssd.reshape(bsz, s, _W_SSD), p_sm.reshape(bsz, s, _W_SM), dt_bias, a_log, d_skip, norm_w,
      jnp.asarray(e8, BF16)).reshape(m, D_BRANCH)


def _mlstm_kernel(p_ref, sm_ref, ib_ref, fb_ref, nw_ref, jm_ref, o_ref, c_ref, m_ref):
    @pl.when(pl.program_id(0) == 0)
    def _():
        c_ref[...] = jnp.zeros(c_ref.shape, F32)
        m_ref[...] = jnp.zeros(m_ref.shape, F32)

    nb = p_ref.shape[0]
    rng = range(nb)
    d = D_BRANCH
    tri = _tri_mask()
    tri_bf = jnp.where(tri, 1.0, 0.0).astype(BF16)

    i_log = [sm_ref[b, :, LANES:2 * LANES] + ib_ref[...] for b in rng]
    f_log = [-_softplus(-(sm_ref[b, :, 2 * LANES:3 * LANES] + fb_ref[...])) for b in rng]
    f_cum = _cumsum_rows_batched(tri_bf, f_log)
    r = [i_log[b] - f_cum[b] for b in rng]
    r_t = [r[b].T for b in rng]
    a_col, inter, em, s_old, w_shift = [], [], [], [], []
    for b in rng:
        f_last = f_cum[b][CHUNK - 1:CHUNK, :]
        m_prev = m_ref[b]
        m_new = jnp.maximum(f_last + m_prev, jnp.max(f_last + r[b], axis=0, keepdims=True))
        s_old.append(jnp.exp(f_last + m_prev - m_new))
        w_shift.append(f_last - m_new)
        m_ref[b] = m_new
        g_inter = f_cum[b] + m_prev
        m_t = jnp.maximum(g_inter, f_cum[b] + _cummax_rows(r[b]))
        a_col.append(f_cum[b] - m_t)
        inter.append(jnp.exp(g_inter - m_t))
        em.append(jnp.exp(-m_t))

    ones = jnp.ones((CHUNK, ML_HEAD_DIM), BF16)
    ys = [[] for _ in rng]
    for h in range(ML_HEADS):
        c0 = h * ML_HEAD_DIM
        qh = [p_ref[b, :, d + c0:d + c0 + ML_HEAD_DIM] for b in rng]
        kh = [p_ref[b, :, 2 * d + c0:2 * d + c0 + ML_HEAD_DIM] for b in rng]
        v_ext = [jnp.concatenate([p_ref[b, :, 3 * d + c0:3 * d + c0 + ML_HEAD_DIM], ones], axis=1)
                 for b in rng]
        qk = [_dot_nt(qh[b], kh[b]) for b in rng]
        qc = [_dot(qh[b], c_ref[b, h].astype(BF16)) for b in rng]
        sm = []
        for b in rng:
            dmat = jnp.where(tri, a_col[b][:, h:h + 1] + r_t[b][h:h + 1, :], NEG_BIG)
            sm.append((qk[b] * jnp.exp(dmat)).astype(BF16))
        numden = [_dot(sm[b], v_ext[b]) + inter[b][:, h:h + 1] * qc[b] for b in rng]
        kw_t = [kh[b].T * jnp.exp(r_t[b][h:h + 1, :] + w_shift[b][:, h:h + 1]).astype(BF16)
                for b in rng]
        upd = [_dot(kw_t[b], v_ext[b]) for b in rng]
        for b in rng:
            num = numden[b][:, 0:ML_HEAD_DIM]
            den = numden[b][:, ML_HEAD_DIM:]
            o_gate = p_ref[b, :, 4 * d + c0:4 * d + c0 + ML_HEAD_DIM].astype(F32)
            ys[b].append(o_gate * (num / jnp.maximum(jnp.abs(den), em[b][:, h:h + 1])))
            c_ref[b, h] = s_old[b][:, h:h + 1] * c_ref[b, h] + upd[b]

    y = [jnp.concatenate(ys[b], axis=1) for b in rng]
    ms = [_dot((y[b] * y[b]).astype(BF16), jm_ref[...]) for b in rng]
    for b in rng:
        gate = p_ref[b, :, 0:d].astype(F32)
        o_ref[b] = (y[b] * lax.rsqrt(ms[b] + NORM_EPS) * nw_ref[...] * gate).astype(o_ref.dtype)


def _mlstm_call(l, p_ml, p_sm, i_bias, f_bias, norm_w, bsz, s):
    nc = s // CHUNK
    m = bsz * s
    row = lambda n: pl.BlockSpec((bsz, CHUNK, n), lambda c: (0, c, 0))
    jm = np.kron(np.eye(ML_HEADS), np.full((ML_HEAD_DIM, ML_HEAD_DIM), 1.0 / ML_HEAD_DIM))
    return pl.pallas_call(
        _mlstm_kernel,
        grid=(nc,),
        in_specs=[row(_W_ML), row(_W_SM),
                  _layer_spec(l, (1, LANES), 2), _layer_spec(l, (1, LANES), 2),
                  _layer_spec(l, (1, D_BRANCH), 2),
                  pl.BlockSpec((D_BRANCH, D_BRANCH), lambda c: (0, 0))],
        out_specs=row(D_BRANCH),
        out_shape=jax.ShapeDtypeStruct((bsz, s, D_BRANCH), BF16),
        scratch_shapes=[pltpu.VMEM((bsz, ML_HEADS, ML_HEAD_DIM, 2 * ML_HEAD_DIM), F32),
                        pltpu.VMEM((bsz, 1, LANES), F32)],
        compiler_params=_cp(("arbitrary",)),
        name="mlstm_mixer",
    )(p_ml.reshape(bsz, s, _W_ML), p_sm.reshape(bsz, s, _W_SM), i_bias, f_bias,
      norm_w, jnp.asarray(jm, BF16)).reshape(m, D_BRANCH)


def _ret_log_gamma():
    return [math.log1p(-2.0 ** (-(RET_DECAY_BASE + h))) for h in range(RET_HEADS)]


def _ret_consts():
    lg = _ret_log_gamma()
    pos = np.arange(CHUNK, dtype=np.float64)
    rel = pos[:, None] - pos[None, :]
    dm = np.stack([np.where(rel >= 0, np.exp(np.maximum(rel, 0.0) * g), 0.0) for g in lg])
    from_start = np.concatenate(
        [np.repeat(np.exp((pos + 1.0) * g)[:, None], RET_V, axis=1) for g in lg], axis=1)
    to_end = np.concatenate(
        [np.repeat(np.exp((CHUNK - 1.0 - pos) * g)[:, None], RET_QK, axis=1) for g in lg], axis=1)
    qw = RET_HEADS * RET_QK
    swap = np.zeros((qw, qw))
    for j in range(qw):
        base, off = (j // RET_QK) * RET_QK, j % RET_QK
        swap[base + (off + RET_QK // 2) % RET_QK, j] = 1.0
    f = lambda a: jnp.asarray(a, F32)
    return f(dm), f(from_start), f(to_end), jnp.asarray(swap, BF16)


def _ret_kernel(p_ref, cos_ref, sin_ref, dm_ref, fs_ref, te_ref, sw_ref, nw_ref, o_ref, r_ref):
    @pl.when(pl.program_id(0) == 0)
    def _():
        r_ref[...] = jnp.zeros(r_ref.shape, F32)

    nb = p_ref.shape[0]
    d = D_BRANCH
    qw = RET_HEADS * RET_QK
    sw = sw_ref[...]
    lane = lax.broadcasted_iota(jnp.int32, (CHUNK, qw), 1)
    head_masks = [(lane >= h * RET_QK) & (lane < (h + 1) * RET_QK) for h in range(RET_HEADS)]

    def v_of(b):
        return p_ref[b, :, d + 2 * qw:2 * d + 2 * qw]

    q_raw = [p_ref[b, :, d:d + qw] for b in range(nb)]
    k_raw = [p_ref[b, :, d + qw:d + 2 * qw] for b in range(nb)]
    q_sw = [_dot(q_raw[b], sw) for b in range(nb)]
    k_sw = [_dot(k_raw[b], sw) for b in range(nb)]
    qs, ks = [], []
    for b in range(nb):
        cos_e = jnp.concatenate([cos_ref[b]] * (qw // LANES), axis=1)
        sin_e = jnp.concatenate([sin_ref[b]] * (qw // LANES), axis=1)
        qs.append(q_raw[b].astype(F32) * cos_e + q_sw[b] * sin_e)
        ks.append((k_raw[b].astype(F32) * cos_e + k_sw[b] * sin_e) * (RET_QK ** -0.5))
    k_bf = [k.astype(BF16) for k in ks]
    q_bf = [q.astype(BF16) for q in qs]
    def state_operand(b):
        rows = []
        for h in range(RET_HEADS):
            parts = [jnp.zeros((RET_QK, RET_V), BF16)] * RET_HEADS
            parts[h] = r_ref[b, h].astype(BF16)
            rows.append(jnp.concatenate(parts, axis=1))
        return jnp.concatenate(rows, axis=0)

    ys = [_dot(q_bf[b], state_operand(b)) * fs_ref[...] for b in range(nb)]
    inner = [[] for _ in range(nb)]
    for h in range(RET_HEADS):
        scores = [(_dot_nt(jnp.where(head_masks[h], q_bf[b], jnp.zeros_like(q_bf[b])), k_bf[b])
                   * dm_ref[h]).astype(BF16) for b in range(nb)]
        for b in range(nb):
            inner[b].append(_dot(scores[b], v_of(b)[:, h * RET_V:(h + 1) * RET_V]))
    ys = [ys[b] + jnp.concatenate(inner[b], axis=1) for b in range(nb)]
    upd = [_dot_tn((ks[b] * te_ref[...]).astype(BF16), v_of(b)) for b in range(nb)]
    chunk_decay = [math.exp(CHUNK * g) for g in _ret_log_gamma()]
    for b in range(nb):
        for h in range(RET_HEADS):
            blk = upd[b][h * RET_QK:(h + 1) * RET_QK, h * RET_V:(h + 1) * RET_V]
            r_ref[b, h] = r_ref[b, h] * chunk_decay[h] + blk
    for b in range(nb):
        gate = p_ref[b, :, 0:d].astype(F32)
        outs = [_rms(ys[b][:, h * RET_V:(h + 1) * RET_V]) for h in range(RET_HEADS)]
        o_ref[b] = (jnp.concatenate(outs, axis=1) * nw_ref[...] * gate).astype(o_ref.dtype)


def _ret_call(l, p_ret, cos_t, sin_t, consts, norm_w, bsz, s):
    nc = s // CHUNK
    m = bsz * s
    qw, vw = RET_HEADS * RET_QK, RET_HEADS * RET_V
    dm, fs, te, sw = consts
    const = lambda shape: pl.BlockSpec(shape, lambda c: (0,) * len(shape))
    row = lambda n: pl.BlockSpec((bsz, CHUNK, n), lambda c: (0, c, 0))
    return pl.pallas_call(
        _ret_kernel,
        grid=(nc,),
        in_specs=[row(_W_RET), row(LANES), row(LANES),
                  const((RET_HEADS, CHUNK, CHUNK)), const((CHUNK, vw)), const((CHUNK, qw)),
                  const((qw, qw)), _layer_spec(l, (1, D_BRANCH), 2)],
        out_specs=row(D_BRANCH),
        out_shape=jax.ShapeDtypeStruct((bsz, s, D_BRANCH), BF16),
        scratch_shapes=[pltpu.VMEM((bsz, RET_HEADS, RET_QK, RET_V), F32)],
        compiler_params=_cp(("arbitrary",)),
        name="retention_mixer",
    )(p_ret.reshape(bsz, s, _W_RET), cos_t.reshape(bsz, s, LANES), sin_t.reshape(bsz, s, LANES),
      dm, fs, te, sw, norm_w).reshape(m, D_BRANCH)


def _s5_compact(lam_re, lam_im, b_re, b_im, c_re, c_im, log_step):
    depth = lam_re.shape[0]
    nb, ngl, sub = S5_LANE_BLOCKS, S5_LOCAL_GROUPS, S5_SUB
    step = jnp.exp(log_step.astype(F32))[..., None]
    lr = jnp.minimum(lam_re.astype(F32), -1e-4)
    li = lam_im.astype(F32)
    mag = jnp.exp(lr * step)
    ang = li * step
    ab_re = mag * jnp.cos(ang)
    ab_im = mag * jnp.sin(ang)
    den = lr * lr + li * li
    coef_re = ((ab_re - 1.0) * lr + ab_im * li) / den
    coef_im = (ab_im * lr - (ab_re - 1.0) * li) / den
    br_t, bi_t = jnp.swapaxes(b_re.astype(F32), -1, -2), jnp.swapaxes(b_im.astype(F32), -1, -2)
    cf_re, cf_im = coef_re[:, :, None, :], coef_im[:, :, None, :]
    bb_re = cf_re * br_t - cf_im * bi_t
    bb_im = cf_re * bi_t + cf_im * br_t
    pw_re, pw_im = [jnp.ones_like(ab_re)], [jnp.zeros_like(ab_im)]
    for _ in range(sub):
        pr, pi = pw_re[-1], pw_im[-1]
        pw_re.append(pr * ab_re - pi * ab_im)
        pw_im.append(pr * ab_im + pi * ab_re)
    cr, ci = c_re.astype(F32), c_im.astype(F32)
    cr_t, ci_t = jnp.swapaxes(cr, -1, -2), jnp.swapaxes(ci, -1, -2)
    hp = lax.Precision.HIGHEST
    abb_re = [pw_re[k][:, :, None, :] * bb_re - pw_im[k][:, :, None, :] * bb_im for k in range(sub)]
    abb_im = [pw_re[k][:, :, None, :] * bb_im + pw_im[k][:, :, None, :] * bb_re for k in range(sub)]
    taps_t = [jnp.einsum('lgip,lgop->lgio', abb_re[k], cr, precision=hp)
              - jnp.einsum('lgip,lgop->lgio', abb_im[k], ci, precision=hp) for k in range(sub)]

    def by_block(rows):
        a = jnp.stack(rows).reshape(len(rows), depth, nb, ngl, rows[0].shape[-2], rows[0].shape[-1])
        return a.transpose(1, 2, 0, 3, 4, 5).reshape(depth, nb, -1, rows[0].shape[-1])

    c_in = jnp.concatenate([by_block([abb_re[sub - 1 - t] for t in range(sub)]),
                            by_block([abb_im[sub - 1 - t] for t in range(sub)])], axis=-1)
    zero = jnp.zeros_like(taps_t[0])
    toe = by_block([jnp.concatenate([zero] * t + taps_t[:sub - t], axis=-1) for t in range(sub)])

    def rows_x(c_t, s_t, sign):
        cols = [c_t * pw_re[k][..., None] - sign * s_t * pw_im[k][..., None] for k in range(1, sub + 1)]
        return jnp.stack(cols, axis=-2).reshape(depth, nb, ngl * S5_STATE, sub * S5_GROUP)

    x_re = rows_x(cr_t, ci_t, 1.0)
    x_im = -(jnp.stack([cr_t * pw_im[k][..., None] + ci_t * pw_re[k][..., None]
                        for k in range(1, sub + 1)], axis=-2)
             .reshape(depth, nb, ngl * S5_STATE, sub * S5_GROUP))
    c_out = jnp.concatenate([toe, x_re, x_im], axis=2)
    a8 = jnp.concatenate([pw_re[sub].reshape(depth, nb, 1, ngl * S5_STATE),
                          pw_im[sub].reshape(depth, nb, 1, ngl * S5_STATE)], axis=-1)
    return c_in.astype(BF16), c_out.astype(BF16), a8


def _s5_expanders():
    kin = S5_SUB * LANES
    half = S5_LOCAL_GROUPS * S5_STATE
    sel_in = np.zeros((LANES, 2 * half), np.float32)
    csel_in = np.zeros((LANES, 2 * half), np.float32)
    for col in range(2 * half):
        sel_in[(col // half) * S5_STATE + col % S5_STATE, col] = 1.0
        csel_in[(col % half) // S5_STATE, col] = 1.0
    sel_out = np.zeros((LANES, kin), np.float32)
    csel_out = np.zeros((LANES, kin), np.float32)
    for col in range(kin):
        sel_out[(col // LANES) * S5_GROUP + col % S5_GROUP, col] = 1.0
        csel_out[(col // S5_GROUP) % S5_LOCAL_GROUPS, col] = 1.0
    rsel_in = np.zeros((kin, LANES), np.float32)
    for r in range(kin):
        rsel_in[r, (r // S5_GROUP) % S5_LOCAL_GROUPS] = 1.0
    rsel_out = np.zeros((2 * kin, LANES), np.float32)
    rsel_out[:kin] = rsel_in
    for r in range(kin):
        rsel_out[kin + r, (r % half) // S5_STATE] = 1.0
    return tuple(jnp.asarray(a, BF16) for a in (sel_in, csel_in, rsel_in, sel_out, csel_out, rsel_out))


def _s5_fold(a, bsz, s):
    j = s // S5_SUB
    a = a.reshape(bsz, j, S5_SUB, S5_LANE_BLOCKS, LANES).transpose(1, 0, 3, 2, 4)
    return a.reshape(j * bsz, S5_LANE_BLOCKS * S5_SUB * LANES)


def _s5_unfold(a, bsz, s):
    j = s // S5_SUB
    a = a.reshape(j, bsz, S5_LANE_BLOCKS, S5_SUB, LANES).transpose(1, 0, 3, 2, 4)
    return a.reshape(bsz * s, D_BRANCH)


S5_ROW_BLOCK = 512


def _s5_scan_kernel(u_ref, cin_ref, cout_ref, a8_ref, sin_ref, csin_ref, rsin_ref,
                    sout_ref, csout_ref, rsout_ref, o_ref, x_ref, min_ref, mout_ref, *, bsz):
    rows = u_ref.shape[0]
    half = S5_LOCAL_GROUPS * S5_STATE
    rb = min(S5_ROW_BLOCK, rows)

    min_ref[...] = (_dot(cin_ref[...], sin_ref[...]) * _dot(rsin_ref[...], csin_ref[...])).astype(BF16)
    mout_ref[...] = (_dot(cout_ref[...], sout_ref[...]) * _dot(rsout_ref[...], csout_ref[...])).astype(BF16)

    for r0 in range(0, rows, rb):
        x_ref[r0:r0 + rb, :] = _dot(u_ref[r0:r0 + rb, :], min_ref[...])

    a_re = a8_ref[:, 0:half]
    a_im = a8_ref[:, half:]

    def body(i, carry):
        x_re, x_im = carry
        r0 = pl.multiple_of(i * bsz, bsz)
        loc = x_ref[pl.ds(r0, bsz), :]
        x_ref[pl.ds(r0, bsz), :] = jnp.concatenate([x_re, x_im], axis=1)
        n_re = a_re * x_re - a_im * x_im + loc[:, 0:half]
        n_im = a_re * x_im + a_im * x_re + loc[:, half:]
        return n_re, n_im

    zero = jnp.zeros((bsz, half), F32)
    lax.fori_loop(0, rows // bsz, body, (zero, zero), unroll=8)

    kin = u_ref.shape[1]
    nt_w = 2 * LANES
    for r0 in range(0, rows, rb):
        xb = x_ref[r0:r0 + rb, :].astype(BF16)
        for c0 in range(0, kin, nt_w):
            k_u = c0 + nt_w
            y = _dot(u_ref[r0:r0 + rb, 0:k_u], mout_ref[0:k_u, c0:c0 + nt_w])
            y = y + _dot(xb, mout_ref[kin:2 * kin, c0:c0 + nt_w])
            o_ref[r0:r0 + rb, c0:c0 + nt_w] = y.astype(o_ref.dtype)


def _s5_scan_call(l, u_fold, c_in, c_out, a8, expanders, bsz):
    rows = u_fold.shape[0]
    kin = S5_SUB * LANES
    lspec = lambda shape: pl.BlockSpec((None, None) + shape, lambda lb: (l, lb, 0, 0))
    const = lambda a: pl.BlockSpec(a.shape, lambda lb: (0, 0))
    sel_in, csel_in, rsel_in, sel_out, csel_out, rsel_out = expanders
    return pl.pallas_call(
        functools.partial(_s5_scan_kernel, bsz=bsz),
        grid=(S5_LANE_BLOCKS,),
        in_specs=[pl.BlockSpec((rows, kin), lambda lb: (0, lb)),
                  lspec((kin, LANES)), lspec((2 * kin, LANES)), lspec((1, kin)),
                  const(sel_in), const(csel_in), const(rsel_in),
                  const(sel_out), const(csel_out), const(rsel_out)],
        out_specs=pl.BlockSpec((rows, kin), lambda lb: (0, lb)),
        out_shape=jax.ShapeDtypeStruct(u_fold.shape, BF16),
        scratch_shapes=[pltpu.VMEM((rows, kin), F32),
                        pltpu.VMEM((kin, kin), BF16), pltpu.VMEM((2 * kin, kin), BF16)],
        compiler_params=_cp(("parallel",)),
        name="s5_scan",
    )(u_fold, c_in, c_out, a8, sel_in, csel_in, rsel_in, sel_out, csel_out, rsel_out)


def _outproj_kernel(y_ssd, y_ml, s5g_ref, s5u_ref, s5y_ref, y_ret, w_ref, h_ref, g_ref, fw_ref,
                    d_ref, wg_ref, bg_ref, nw5_ref, o_ref, *, final_norm):
    d = D_BRANCH
    u = s5u_ref[...].astype(F32)
    y = s5y_ref[...].astype(F32) + d_ref[...] * u
    y = 0.5 * y * (1.0 + jnp.tanh(math.sqrt(2.0 / math.pi) * (y + 0.044715 * (y * y * y))))
    g = _dot(y.astype(BF16), wg_ref[...]) + bg_ref[...]
    y = g[:, 0:d] * _sigmoid(g[:, d:2 * d])
    y_s5 = (_rms(y) * nw5_ref[...] * s5g_ref[...].astype(F32)).astype(BF16)

    acc = _dot(y_ssd[...], w_ref[0 * d:1 * d, :])
    acc = acc + _dot(y_ml[...], w_ref[1 * d:2 * d, :])
    acc = acc + _dot(y_s5, w_ref[2 * d:3 * d, :])
    acc = acc + _dot(y_ret[...], w_ref[3 * d:4 * d, :])
    h = h_ref[...] + g_ref[...] * acc
    if final_norm:
        h = _rms(h) * fw_ref[...]
    o_ref[...] = h


def _outproj_call(l, y_ssd, y_ml, p_s5g, p_s5u, y_ssm, y_ret, w_out, h, mod5, final_w,
                  s5_d, w_glu, b_glu, s5_nw, s, final_norm):
    m = h.shape[0]
    tm = min(2 * ROW_TILE, s)
    per_batch = s // tm
    yspec = pl.BlockSpec((tm, D_BRANCH), lambda i: (i, 0))
    return pl.pallas_call(
        functools.partial(_outproj_kernel, final_norm=final_norm),
        grid=(m // tm,),
        in_specs=[yspec] * 6 + [
            _layer_spec(l, (4 * D_BRANCH, D_MODEL), 1),
            pl.BlockSpec((tm, D_MODEL), lambda i: (i, 0)),
            pl.BlockSpec((None, None, None, 1, D_MODEL), lambda i: (l, i // per_batch, 2, 0, 0)),
            pl.BlockSpec((1, D_MODEL), lambda i: (0, 0)),
            _layer_spec(l, (1, D_BRANCH), 1), _layer_spec(l, (D_BRANCH, 2 * D_BRANCH), 1),
            _layer_spec(l, (1, 2 * D_BRANCH), 1), _layer_spec(l, (1, D_BRANCH), 1),
        ],
        out_specs=pl.BlockSpec((tm, D_MODEL), lambda i: (i, 0)),
        out_shape=jax.ShapeDtypeStruct((m, D_MODEL), F32),
        compiler_params=_cp(("parallel",)),
        name="out_proj",
    )(y_ssd, y_ml, p_s5g, p_s5u, y_ssm, y_ret, w_out, h, mod5, final_w, s5_d, w_glu, b_glu, s5_nw)


def _pack_w_in(w_in):
    assert w_in.shape[-1] == _D_IN
    pad = lambda a: jnp.pad(a, [(0, 0)] * (a.ndim - 1) + [(0, LANES - a.shape[-1])])
    parts = [w_in[..., _O_SSD_Z:_O_SSD_DT], w_in[..., _O_ML_Z:_O_ML_I], w_in[..., _O_S5_Z:_O_RET_Z],
             w_in[..., _O_RET_Z:_D_IN], pad(w_in[..., _O_SSD_DT:_O_ML_Z]),
             pad(w_in[..., _O_ML_I:_O_ML_F]), pad(w_in[..., _O_ML_F:_O_S5_Z])]
    return jnp.concatenate(parts, axis=-1).astype(BF16)


def _pad_lanes(v):
    return jnp.pad(v.astype(F32), ((0, 0), (0, LANES - v.shape[-1])))[:, None, :]


def kernel(x, c, positions, norm_w, w_ada, b_ada, w_in, w_out, ssd_conv_w, ssd_conv_b, ssd_dt_bias, ssd_a_log, ssd_d, ssd_norm_w, ml_conv_w, ml_conv_b, ml_i_bias, ml_f_bias, ml_norm_w, s5_lambda_re, s5_lambda_im, s5_b_re, s5_b_im, s5_c_re, s5_c_im, s5_d, s5_log_step, s5_w_glu, s5_b_glu, s5_norm_w, ret_norm_w, final_norm_w):
    bsz, s, d = x.shape
    depth = w_in.shape[0]
    assert d == D_MODEL and s % CHUNK == 0 and bsz % 8 == 0
    m = bsz * s
    row3 = lambda a: a.astype(F32)[:, None, :]

    mod = _mod_call(c.astype(F32), w_ada, b_ada)
    mod5 = mod.reshape(depth, bsz, 3, 1, D_MODEL)
    cos_t, sin_t = _rope_call(positions)
    ret_consts = _ret_consts()
    w_all = _pack_w_in(w_in)
    w_out_bf = w_out.astype(BF16)
    w_glu_bf = s5_w_glu.astype(BF16)
    c_in, c_out, a8 = _s5_compact(s5_lambda_re, s5_lambda_im, s5_b_re, s5_b_im, s5_c_re, s5_c_im,
                                  s5_log_step)
    s5_exp = _s5_expanders()
    norm_w3, final_w = row3(norm_w), final_norm_w.reshape(1, D_MODEL).astype(F32)
    ssd_cb3, ml_cb3 = row3(ssd_conv_b), row3(ml_conv_b)
    dt_bias3, a_log3 = _pad_lanes(ssd_dt_bias), _pad_lanes(ssd_a_log)
    d_skip3 = row3(jnp.repeat(ssd_d, SSD_HEAD_DIM, axis=-1))
    i_bias3, f_bias3 = _pad_lanes(ml_i_bias), _pad_lanes(ml_f_bias)
    ssd_nw3, ml_nw3, s5_nw3, ret_nw3 = row3(ssd_norm_w), row3(ml_norm_w), row3(s5_norm_w), row3(ret_norm_w)
    s5_d3, b_glu3 = row3(s5_d), row3(s5_b_glu)

    h = x.reshape(m, D_MODEL)
    for l in range(depth):
        p_ssd, p_ml, p_s5g, p_s5u, p_ret, p_sm = _inproj_call(
            l, h, norm_w3, mod5, w_all, ssd_conv_w, ssd_cb3, ml_conv_w, ml_cb3, s)
        y_ssd = _ssd_call(l, p_ssd, p_sm, dt_bias3, a_log3, d_skip3, ssd_nw3, bsz, s)
        y_ml = _mlstm_call(l, p_ml, p_sm, i_bias3, f_bias3, ml_nw3, bsz, s)
        y_fold = _s5_scan_call(l, _s5_fold(p_s5u, bsz, s), c_in, c_out, a8, s5_exp, bsz)
        y_ssm = _s5_unfold(y_fold, bsz, s)
        y_ret = _ret_call(l, p_ret, cos_t, sin_t, ret_consts, ret_nw3, bsz, s)
        h = _outproj_call(l, y_ssd, y_ml, p_s5g, p_s5u, y_ssm, y_ret, w_out_bf, h, mod5, final_w,
                          s5_d3, w_glu_bf, b_glu3, s5_nw3, s, final_norm=(l == depth - 1))
    return h.reshape(bsz, s, D_MODEL)
```
